```python
import math
import jax, jax.numpy as jnp
from jax import lax
import numpy as np

D_MODEL = 1024
BATCH = 16
SEQ = 2048
DEPTH = 4

HEAD_DIM_A = 128
KV_HEADS_A = 4
DILATED_GROUPS = ((128, 1), (512, 4), (2048, 16))
N_GROUPS_A = len(DILATED_GROUPS)
Q_HEADS_A = KV_HEADS_A * N_GROUPS_A
ROPE_THETA = 10000.0
GLA_HEADS = 4
GLA_DK = D_MODEL // 2 // GLA_HEADS
GLA_DV = D_MODEL // GLA_HEADS
GLA_LOWRANK = 16
GLA_GATE_TEMP = 16.0
GLA_CHUNK = 64
D_FF = -(-8 * D_MODEL // (3 * 256)) * 256
EPS = 1e-6

W_QA = Q_HEADS_A * HEAD_DIM_A
W_KA = KV_HEADS_A * HEAD_DIM_A
W_VA = KV_HEADS_A * HEAD_DIM_A
W_QB = GLA_HEADS * GLA_DK
W_KB = GLA_HEADS * GLA_DK
W_VB = GLA_HEADS * GLA_DV
W_RB = GLA_HEADS * GLA_DV
IN_SPLITS = (W_QA, W_KA, W_VA, W_QB, W_KB, W_VB, W_RB, GLA_LOWRANK, D_MODEL, D_MODEL)
D_IN = sum(IN_SPLITS)
D_OUT_A = W_VA
D_OUT_B = W_VB

kernel_name = "hybrid_dilated_attn_gla_swiglu"


def rmsnorm(x, g):
    xf = x.astype(jnp.float32)
    y = xf * lax.rsqrt(jnp.mean(xf * xf, axis=-1, keepdims=True) + EPS) * g.astype(jnp.float32)
    return y.astype(x.dtype)


def rope_tables(positions):
    inv_freq = ROPE_THETA ** (-jnp.arange(0, HEAD_DIM_A, 2, dtype=jnp.float32) / HEAD_DIM_A)
    ang = positions.astype(jnp.float32)[..., None] * inv_freq
    return jnp.cos(ang)[:, :, None, :], jnp.sin(ang)[:, :, None, :]


def apply_rope(t, cos, sin):
    tf = t.astype(jnp.float32)
    t1, t2 = jnp.split(tf, 2, axis=-1)
    return jnp.concatenate([t1 * cos - t2 * sin, t2 * cos + t1 * sin], axis=-1).astype(t.dtype)


def dilated_group(q, k, v, window, dilation):
    B, S, H, hd = q.shape
    nk = window // dilation
    L = S // dilation
    nb = -(-L // nk)
    Lp = nb * nk

    def to_blocks(t):
        t = t.astype(jnp.float32).reshape(B, L, dilation, H, hd).transpose(0, 2, 3, 1, 4)
        t = jnp.pad(t, ((0, 0), (0, 0), (0, 0), (0, Lp - L), (0, 0)))
        return t.reshape(B, dilation, H, nb, nk, hd)

    def with_prev(t):
        prev = jnp.pad(t, ((0, 0), (0, 0), (0, 0), (1, 0), (0, 0), (0, 0)))[:, :, :, :-1]
        return jnp.concatenate([prev, t], axis=4)

    qb = to_blocks(q)
    kb = with_prev(to_blocks(k))
    vb = with_prev(to_blocks(v))
    s = jnp.einsum('brhnqe,brhnke->brhnqk', qb, kb)
    qi = jnp.arange(nk)[:, None] + nk
    ki = jnp.arange(2 * nk)[None, :]
    dist = qi - ki
    blk = jnp.arange(nb)[:, None, None]
    valid = (dist >= 0) & (dist <= nk) & (blk * nk + ki - nk >= 0)
    s = jnp.where(valid, s, -jnp.inf)
    m = jnp.max(s, axis=-1, keepdims=True)
    p = jnp.exp(s - m)
    den = jnp.sum(p, axis=-1)
    o = jnp.einsum('brhnqk,brhnke->brhnqe', p, vb) / den[..., None]
    lse = m[..., 0] + jnp.log(den)
    o = o.reshape(B, dilation, H, Lp, hd)[:, :, :, :L].transpose(0, 3, 1, 2, 4).reshape(B, S, H, hd)
    lse = lse.reshape(B, dilation, H, Lp)[:, :, :, :L].transpose(0, 3, 1, 2).reshape(B, S, H)
    return o, lse


def dilated_attention(q, k, v):
    outs, lses = [], []
    for g, (window, dilation) in enumerate(DILATED_GROUPS):
        o, lse = dilated_group(q[:, :, g], k, v, window, dilation)
        outs.append(o)
        lses.append(lse)
    w = jax.nn.softmax(jnp.stack(lses, axis=0), axis=0)
    o = jnp.sum(w[..., None] * jnp.stack(outs, axis=0), axis=0)
    return o.astype(v.dtype)


def gla(q, k, v, log_a):
    B, S, H, dk = q.shape
    dv = v.shape[-1]
    nc = S // GLA_CHUNK

    def to_chunks(t):
        return t.astype(jnp.float32).reshape(B, nc, GLA_CHUNK, H, -1).transpose(1, 0, 3, 2, 4)

    qc, kc, vc = to_chunks(q), to_chunks(k), to_chunks(v)
    bc = jnp.cumsum(to_chunks(log_a), axis=3)
    causal = jnp.tril(jnp.ones((GLA_CHUNK, GLA_CHUNK), dtype=bool))[:, :, None]

    def step(state, xs):
        qi, ki, vi, bi = xs
        b_last = bi[:, :, -1:, :]
        inter = jnp.einsum('bhtd,bhde->bhte', qi * jnp.exp(bi), state)
        diff = bi[:, :, :, None, :] - bi[:, :, None, :, :]
        decay = jnp.exp(jnp.where(causal, diff, -jnp.inf))
        att = jnp.einsum('bhtd,bhsd,bhtsd->bhts', qi, ki, decay)
        intra = jnp.einsum('bhts,bhse->bhte', att, vi)
        new_state = jnp.exp(b_last[:, :, 0, :])[..., None] * state + jnp.einsum(
            'bhsd,bhse->bhde', ki * jnp.exp(b_last - bi), vi)
        return new_state, inter + intra

    state0 = jnp.zeros((B, H, dk, dv), jnp.float32)
    _, o = lax.scan(step, state0, (qc, kc, vc, bc))
    return o.transpose(1, 0, 3, 2, 4).reshape(B, S, H, dv).astype(v.dtype)


def hybrid_layer(x, cos, sin, norm1, w_in, qn_a, kn_a, w_a_up, b_a, gn_b,
                 w_proj_a, w_proj_b, w_out, norm2, w_ffn_gate, w_ffn_up, w_ffn_down):
    B, S, _ = x.shape
    h = rmsnorm(x, norm1)
    proj = h @ w_in
    offsets = np.cumsum(IN_SPLITS)[:-1].tolist()
    qa, ka, va, qb, kb, vb, rb, ab, ga, gb = jnp.split(proj, offsets, axis=-1)

    qa = apply_rope(rmsnorm(qa.reshape(B, S, Q_HEADS_A, HEAD_DIM_A), qn_a), cos, sin) * (HEAD_DIM_A ** -0.5)
    ka = apply_rope(rmsnorm(ka.reshape(B, S, KV_HEADS_A, HEAD_DIM_A), kn_a), cos, sin)
    va = va.reshape(B, S, KV_HEADS_A, HEAD_DIM_A)
    oa = dilated_attention(qa.reshape(B, S, N_GROUPS_A, KV_HEADS_A, HEAD_DIM_A), ka, va)
    oa = oa.reshape(B, S, D_OUT_A)

    qb = qb.reshape(B, S, GLA_HEADS, GLA_DK) * (GLA_DK ** -0.5)
    kb = kb.reshape(B, S, GLA_HEADS, GLA_DK)
    vb = vb.reshape(B, S, GLA_HEADS, GLA_DV)
    log_a = jax.nn.log_sigmoid((ab @ w_a_up + b_a).astype(jnp.float32)) / GLA_GATE_TEMP
    ob = gla(qb, kb, vb, log_a.reshape(B, S, GLA_HEADS, GLA_DK))
    ob = rmsnorm(ob, gn_b) * jax.nn.silu(rb.reshape(B, S, GLA_HEADS, GLA_DV))
    ob = ob.reshape(B, S, D_OUT_B)

    y = jax.nn.sigmoid(ga) * (oa @ w_proj_a) + jax.nn.sigmoid(gb) * (ob @ w_proj_b)
    x = x + y @ w_out

    h2 = rmsnorm(x, norm2)
    x = x + (jax.nn.silu(h2 @ w_ffn_gate) * (h2 @ w_ffn_up)) @ w_ffn_down
    return x


def setup_inputs(seed: int = 0) -> dict:
    key = jax.random.key(seed)
    ks = jax.random.split(key, 20)
    f32 = jnp.float32

    def nrm(k, shape, scale):
        return jax.random.normal(k, shape, f32) * scale

    L = DEPTH
    return {
        "x": jax.random.normal(ks[0], (BATCH, SEQ, D_MODEL), f32),
        "positions": jax.random.randint(ks[1], (BATCH, 1), 0, 4096, dtype=jnp.int32)
                     + jnp.arange(SEQ, dtype=jnp.int32)[None, :],
        "norm1": 1.0 + nrm(ks[2], (L, D_MODEL), 0.05),
        "w_in": nrm(ks[3], (L, D_MODEL, D_IN), D_MODEL ** -0.5),
        "qn_a": 1.0 + nrm(ks[4], (L, HEAD_DIM_A), 0.05),
        "kn_a": 1.0 + nrm(ks[5], (L, HEAD_DIM_A), 0.05),
        "w_a_up": nrm(ks[6], (L, GLA_LOWRANK, W_KB), GLA_LOWRANK ** -0.5),
        "b_a": nrm(ks[7], (L, W_KB), 0.1),
        "gn_b": 1.0 + nrm(ks[8], (L, GLA_DV), 0.05),
        "w_proj_a": nrm(ks[9], (L, D_OUT_A, D_MODEL), D_OUT_A ** -0.5),
        "w_proj_b": nrm(ks[10], (L, D_OUT_B, D_MODEL), D_OUT_B ** -0.5),
        "w_out": nrm(ks[11], (L, D_MODEL, D_MODEL), 0.5 * D_MODEL ** -0.5),
        "norm2": 1.0 + nrm(ks[12], (L, D_MODEL), 0.05),
        "w_ffn_gate": nrm(ks[13], (L, D_MODEL, D_FF), D_MODEL ** -0.5),
        "w_ffn_up": nrm(ks[14], (L, D_MODEL, D_FF), D_MODEL ** -0.5),
        "w_ffn_down": nrm(ks[15], (L, D_FF, D_MODEL), 0.5 * D_FF ** -0.5),
    }


def reference(x, positions, norm1, w_in, qn_a, kn_a, w_a_up, b_a, gn_b, w_proj_a, w_proj_b,
              w_out, norm2, w_ffn_gate, w_ffn_up, w_ffn_down):
    cos, sin = rope_tables(positions)
    for l in range(DEPTH):
        x = hybrid_layer(x, cos, sin, norm1[l], w_in[l], qn_a[l], kn_a[l], w_a_up[l], b_a[l], gn_b[l],
                         w_proj_a[l], w_proj_b[l], w_out[l], norm2[l], w_ffn_gate[l], w_ffn_up[l],
                         w_ffn_down[l])
    return x
```

```python
import functools

import jax
import jax.numpy as jnp
from jax import lax
from jax.experimental import pallas as pl
from jax.experimental.pallas import tpu as pltpu

F32 = jnp.float32
BF16 = jnp.bfloat16

D_MODEL = 1024
HEAD_DIM = 128
KV_HEADS = 4
DILATIONS = (1, 4, 16)
KEYS_PER_BLOCK = 128
Q_HEADS = KV_HEADS * len(DILATIONS)
ROPE_THETA = 10000.0
GLA_HEADS = 4
GLA_DK = 128
GLA_DV = 256
GLA_LOWRANK = 16
GLA_GATE_TEMP = 16.0
GLA_CHUNK = 64
GLA_SUB = 16
D_FF = 2816
EPS = 1e-6

W_QA = Q_HEADS * HEAD_DIM
W_KA = KV_HEADS * HEAD_DIM
W_QB = GLA_HEADS * GLA_DK
W_VB = GLA_HEADS * GLA_DV
IN_SPLITS = (W_QA, W_KA, W_KA, W_QB, W_QB, W_VB, W_VB, GLA_LOWRANK, D_MODEL, D_MODEL)

LANES = 128
VMEM_LIMIT_BYTES = 56 * 1024 * 1024


def _dot(a, b):
    return jnp.dot(a, b, preferred_element_type=F32)


def _dot_nt(a, b):
    return lax.dot_general(a, b, (((1,), (1,)), ((), ())), preferred_element_type=F32)


def _dot_tn(a, b):
    return lax.dot_general(a, b, (((0,), (0,)), ((), ())), preferred_element_type=F32)


def _rms(t, gain):
    return t * lax.rsqrt(jnp.mean(t * t, axis=-1, keepdims=True) + EPS) * gain


def _sigmoid(t):
    return 1.0 / (1.0 + jnp.exp(-t))


def _const_spec(shape):
    return pl.BlockSpec(shape, lambda *_: (0,) * len(shape))


def _rope_kernel(pos_ref, invf_ref, sign_ref, cos_ref, sin_ref):
    ang = pos_ref[...].astype(F32) * invf_ref[...]
    cos_ref[...] = jnp.cos(ang)
    sin_ref[...] = jnp.sin(ang) * sign_ref[...]


def _rope_tables(positions):
    n_tok = positions.size
    inv_freq = ROPE_THETA ** (-jnp.arange(0, HEAD_DIM, 2, dtype=F32) / HEAD_DIM)
    invf = jnp.concatenate([inv_freq, inv_freq])[None, :]
    sign = jnp.concatenate([-jnp.ones((HEAD_DIM // 2,), F32), jnp.ones((HEAD_DIM // 2,), F32)])[None, :]
    tm = 2048
    return pl.pallas_call(
        _rope_kernel,
        grid=(n_tok // tm,),
        in_specs=[pl.BlockSpec((tm, 1), lambda i: (i, 0)), _const_spec((1, HEAD_DIM)), _const_spec((1, HEAD_DIM))],
        out_specs=[pl.BlockSpec((tm, HEAD_DIM), lambda i: (i, 0))] * 2,
        out_shape=[jax.ShapeDtypeStruct((n_tok, HEAD_DIM), F32)] * 2,
        name="rope_tables",
    )(positions.reshape(n_tok, 1), invf, sign)


def _inproj_kernel(x_ref, g1_ref, cos_ref, sin_ref, qn_ref, kn_ref, wqa_ref, wka_ref, wva_ref, wqb_ref,
                   wkb_ref, wvb_ref, wrb_ref, wab_ref, wau_ref, ba_ref, wga_ref, wgb_ref,
                   qa_o, ka_o, va_o, qb_o, kb_o, vb_o, rb_o, la_o, ga_o, gb_o):
    h = _rms(x_ref[...], g1_ref[...]).astype(BF16)
    cos = cos_ref[...]
    sin = sin_ref[...]

    def norm_rope(t, gain, scale):
        t = _rms(t, gain)
        t = t * cos + pltpu.roll(t, HEAD_DIM // 2, 1) * sin
        return t * scale

    def per_head(w_ref, out_ref, gain, scale):
        width = w_ref.shape[1]
        step = 4 * HEAD_DIM
        for c0 in range(0, width, step):
            t = _dot(h, w_ref[:, c0:c0 + step])
            for j in range(0, step, HEAD_DIM):
                out_ref[:, c0 + j:c0 + j + HEAD_DIM] = norm_rope(t[:, j:j + HEAD_DIM], gain, scale)

    per_head(wqa_ref, qa_o, qn_ref[...], HEAD_DIM ** -0.5)
    per_head(wka_ref, ka_o, kn_ref[...], 1.0)
    va_o[...] = _dot(h, wva_ref[...])
    qb_o[...] = (_dot(h, wqb_ref[...]) * (GLA_DK ** -0.5)).astype(BF16)
    kb_o[...] = _dot(h, wkb_ref[...]).astype(BF16)
    for c0 in range(0, W_VB, 512):
        vb_o[:, c0:c0 + 512] = _dot(h, wvb_ref[:, c0:c0 + 512]).astype(BF16)
        r = _dot(h, wrb_ref[:, c0:c0 + 512])
        rb_o[:, c0:c0 + 512] = (r * _sigmoid(r)).astype(BF16)
        ga_o[:, c0:c0 + 512] = _sigmoid(_dot(h, wga_ref[:, c0:c0 + 512])).astype(BF16)
        gb_o[:, c0:c0 + 512] = _sigmoid(_dot(h, wgb_ref[:, c0:c0 + 512])).astype(BF16)
    ab = _dot(h, wab_ref[...]).astype(BF16)
    z = _dot(ab, wau_ref[...]) + ba_ref[...]
    log_sig = jnp.minimum(z, 0.0) - jnp.log1p(jnp.exp(-jnp.abs(z)))
    la_o[...] = log_sig * (1.0 / GLA_GATE_TEMP)


def _inproj(x2, cos, sin, g1, qn, kn, w, tm):
    n_tok = x2.shape[0]
    row = lambda width: pl.BlockSpec((tm, width), lambda i: (i, 0))
    weights = [w["qa"], w["ka"], w["va"], w["qb"], w["kb"], w["vb"], w["rb"], w["ab"], w["au"], w["ba"], w["ga"], w["gb"]]
    in_specs = ([row(D_MODEL), _const_spec((1, D_MODEL)), row(HEAD_DIM), row(HEAD_DIM),
                 _const_spec((1, HEAD_DIM)), _const_spec((1, HEAD_DIM))]
                + [_const_spec(a.shape) for a in weights])
    outs = [(W_QA, F32), (W_KA, F32), (W_KA, F32), (W_QB, BF16), (W_QB, BF16), (W_VB, BF16), (W_VB, BF16),
            (W_QB, F32), (D_MODEL, BF16), (D_MODEL, BF16)]
    return pl.pallas_call(
        _inproj_kernel,
        grid=(n_tok // tm,),
        in_specs=in_specs,
        out_specs=[row(width) for width, _ in outs],
        out_shape=[jax.ShapeDtypeStruct((n_tok, width), dt) for width, dt in outs],
        compiler_params=pltpu.CompilerParams(dimension_semantics=("arbitrary",), vmem_limit_bytes=VMEM_LIMIT_BYTES),
        name="inproj",
    )(x2, g1, cos, sin, qn, kn, *weights)


def _attn_kernel(q0_ref, q1_ref, q2_ref, k_ref, v_ref, o_ref, out_scr, lse_scr):
    seq = k_ref.shape[0]
    nk = KEYS_PER_BLOCK
    qi = lax.broadcasted_iota(jnp.int32, (nk, 2 * nk), 0) + nk
    ki = lax.broadcasted_iota(jnp.int32, (nk, 2 * nk), 1)
    dist = qi - ki
    band_mask = (dist >= 0) & (dist <= nk)
    qc = lax.broadcasted_iota(jnp.int32, (nk, nk), 0)
    kc = lax.broadcasted_iota(jnp.int32, (nk, nk), 1)
    causal_mask = qc >= kc

    def rows(start, size, stride):
        return pl.ds(start, size) if stride == 1 else pl.ds(start, size, stride=stride)

    def block(q_ref, group, stride, q_start, k_start, n_keys, mask):
        q = q_ref[rows(q_start, nk, stride), :].astype(BF16)
        k = k_ref[rows(k_start, n_keys, stride), :].astype(BF16)
        v = v_ref[rows(k_start, n_keys, stride), :].astype(BF16)
        s = jnp.where(mask, _dot_nt(q, k), -jnp.inf)
        m = jnp.max(s, axis=-1, keepdims=True)
        p = jnp.exp(s - m)
        den = jnp.sum(p, axis=-1, keepdims=True)
        o = _dot(p.astype(BF16), v) / den
        lse = m + jnp.log(den)
        out_scr[group, rows(q_start, nk, stride), :] = o
        lse_scr[group, rows(q_start, nk, stride), :] = jnp.broadcast_to(lse, (nk, HEAD_DIM))

    for group, (q_ref, stride) in enumerate(zip((q0_ref, q1_ref, q2_ref), DILATIONS)):
        span = nk * stride
        n_blocks = seq // span

        def first(r, carry, q_ref=q_ref, group=group, stride=stride):
            block(q_ref, group, stride, r, r, nk, causal_mask)
            return carry

        lax.fori_loop(0, stride, first, 0)
        if n_blocks > 1:
            def later(i, carry, q_ref=q_ref, group=group, stride=stride, span=span, n_blocks=n_blocks):
                r = i // (n_blocks - 1)
                n = i % (n_blocks - 1) + 1
                block(q_ref, group, stride, r + span * n, r + span * (n - 1), 2 * nk, band_mask)
                return carry

            lax.fori_loop(0, stride * (n_blocks - 1), later, 0)

    chunk = 256

    def combine(i, carry):
        r0 = pl.multiple_of(i * chunk, chunk)
        l0 = lse_scr[0, pl.ds(r0, chunk), :]
        l1 = lse_scr[1, pl.ds(r0, chunk), :]
        l2 = lse_scr[2, pl.ds(r0, chunk), :]
        mx = jnp.maximum(jnp.maximum(l0, l1), l2)
        w0 = jnp.exp(l0 - mx)
        w1 = jnp.exp(l1 - mx)
        w2 = jnp.exp(l2 - mx)
        num = (w0 * out_scr[0, pl.ds(r0, chunk), :] + w1 * out_scr[1, pl.ds(r0, chunk), :]
               + w2 * out_scr[2, pl.ds(r0, chunk), :])
        o_ref[pl.ds(r0, chunk), :] = (num / (w0 + w1 + w2)).astype(o_ref.dtype)
        return carry

    lax.fori_loop(0, seq // chunk, combine, 0)


def _attention(qa, ka, va):
    bsz, seq, _ = ka.shape
    n_groups = len(DILATIONS)

    def q_spec(group):
        return pl.BlockSpec((None, seq, HEAD_DIM), lambda b, h: (b, 0, group * KV_HEADS + h))

    kv_spec = pl.BlockSpec((None, seq, HEAD_DIM), lambda b, h: (b, 0, h))
    return pl.pallas_call(
        _attn_kernel,
        grid=(bsz, KV_HEADS),
        in_specs=[q_spec(0), q_spec(1), q_spec(2), kv_spec, kv_spec],
        out_specs=kv_spec,
        out_shape=jax.ShapeDtypeStruct((bsz, seq, KV_HEADS * HEAD_DIM), BF16),
        scratch_shapes=[pltpu.VMEM((n_groups, seq, HEAD_DIM), F32), pltpu.VMEM((n_groups, seq, HEAD_DIM), F32)],
        compiler_params=pltpu.CompilerParams(dimension_semantics=("arbitrary", "arbitrary"),
                                             vmem_limit_bytes=VMEM_LIMIT_BYTES),
        name="dilated_attention",
    )(qa, qa, qa, ka, va)


def _gla_kernel(q_ref, k_ref, v_ref, la_ref, rb_ref, gn_ref, sel_ref, o_ref,
                b_scr, qh_scr, kh_scr, g_scr, x_scr, d_scr, st_scr):
    seq = q_ref.shape[0]
    sup = 256
    chunks_per_sup = sup // GLA_CHUNK
    subs_per_sup = sup // GLA_SUB

    def prepare(i, carry):
        r0 = pl.multiple_of(i * sup, sup)
        la = la_ref[pl.ds(r0, sup), :]
        row = lax.broadcasted_iota(jnp.int32, (sup, GLA_DK), 0) % GLA_CHUNK
        b = la
        shift = 1
        while shift < GLA_CHUNK:
            b = b + jnp.where(row >= shift, pltpu.roll(b, shift, 0), 0.0)
            shift *= 2
        b_scr[pl.ds(r0, sup), :] = b
        q = q_ref[pl.ds(r0, sup), :].astype(F32)
        k = k_ref[pl.ds(r0, sup), :].astype(F32)
        qh_scr[pl.ds(r0, sup), :] = (q * jnp.exp(b)).astype(BF16)
        b3 = b.reshape(chunks_per_sup, GLA_CHUNK, GLA_DK)
        b_last = b3[:, GLA_CHUNK - 1:GLA_CHUNK, :]
        kh = k.reshape(chunks_per_sup, GLA_CHUNK, GLA_DK) * jnp.exp(b_last - b3)
        kh_scr[pl.ds(r0, sup), :] = kh.reshape(sup, GLA_DK).astype(BF16)
        g_scr[pl.ds(pl.multiple_of(i * chunks_per_sup, chunks_per_sup), chunks_per_sup), :, :] = jnp.broadcast_to(
            jnp.exp(b_last), (chunks_per_sup, 8, GLA_DK))
        q16 = q.reshape(subs_per_sup, GLA_SUB, GLA_DK)
        k16 = k.reshape(subs_per_sup, GLA_SUB, GLA_DK)
        b16 = b.reshape(subs_per_sup, GLA_SUB, GLA_DK)
        off = lax.broadcasted_iota(jnp.int32, (subs_per_sup, GLA_SUB, GLA_DK), 1)
        for j in range(GLA_SUB):
            decay = jnp.exp(jnp.where(off >= j, b16 - b16[:, j:j + 1, :], -jnp.inf))
            xj = q16 * k16[:, j:j + 1, :] * decay
            x_scr[:, j * GLA_DK:(j + 1) * GLA_DK] = xj.reshape(sup, GLA_DK).astype(BF16)
        d_scr[pl.ds(r0, sup), :] = _dot(x_scr[...], sel_ref[...])
        return carry

    lax.fori_loop(0, seq // sup, prepare, 0)

    st_scr[...] = jnp.zeros_like(st_scr)
    rr = lax.broadcasted_iota(jnp.int32, (GLA_CHUNK, GLA_CHUNK), 0)
    cc = lax.broadcasted_iota(jnp.int32, (GLA_CHUNK, GLA_CHUNK), 1)
    same_sub = (rr // GLA_SUB) == (cc // GLA_SUB)
    diag_mask = same_sub & (rr >= cc)
    off_mask = (cc // GLA_SUB) < (rr // GLA_SUB)
    gn = gn_ref[...]

    def chunk_step(c, carry):
        c0 = pl.multiple_of(c * GLA_CHUNK, GLA_CHUNK)
        qc = q_ref[pl.ds(c0, GLA_CHUNK), :].astype(F32)
        kc = k_ref[pl.ds(c0, GLA_CHUNK), :].astype(F32)
        bc = b_scr[pl.ds(c0, GLA_CHUNK), :]
        parts = [jnp.zeros((GLA_SUB, GLA_CHUNK), F32)]
        for sub in range(1, GLA_CHUNK // GLA_SUB):
            lo = sub * GLA_SUB
            b_ref = bc[lo - 1:lo, :]
            q_sub = (qc[lo:lo + GLA_SUB, :] * jnp.exp(bc[lo:lo + GLA_SUB, :] - b_ref)).astype(BF16)
            k_sub = (kc * jnp.exp(jnp.minimum(b_ref - bc, 0.0))).astype(BF16)
            parts.append(_dot_nt(q_sub, k_sub))
        att = jnp.where(off_mask, jnp.concatenate(parts, axis=0), 0.0)
        att = att + jnp.where(diag_mask, d_scr[pl.ds(c0, GLA_CHUNK), :][:, :GLA_CHUNK], 0.0)
        vc = v_ref[pl.ds(c0, GLA_CHUNK), :]
        st = st_scr[...]
        o = _dot_nt(qh_scr[pl.ds(c0, GLA_CHUNK), :], st.astype(BF16)) + _dot(att.astype(BF16), vc)
        st_scr[...] = st * g_scr[c, 0:1, :] + _dot_tn(vc, kh_scr[pl.ds(c0, GLA_CHUNK), :])
        y = _rms(o, gn) * rb_ref[pl.ds(c0, GLA_CHUNK), :].astype(F32)
        o_ref[pl.ds(c0, GLA_CHUNK), :] = y.astype(o_ref.dtype)
        return carry

    lax.fori_loop(0, seq // GLA_CHUNK, chunk_step, 0)


def _gla_selector():
    j = jnp.arange(GLA_SUB * GLA_DK)[:, None] // GLA_DK
    c = jnp.arange(LANES)[None, :]
    return ((c % GLA_SUB == j) & (c < GLA_CHUNK)).astype(BF16)


def _gla(qb, kb, vb, la, rb, gn):
    bsz, seq, _ = qb.shape
    k_spec = pl.BlockSpec((None, seq, GLA_DK), lambda b, h: (b, 0, h))
    v_spec = pl.BlockSpec((None, seq, GLA_DV), lambda b, h: (b, 0, h))
    sel = _gla_selector()
    return pl.pallas_call(
        _gla_kernel,
        grid=(bsz, GLA_HEADS),
        in_specs=[k_spec, k_spec, v_spec, k_spec, v_spec, _const_spec((1, GLA_DV)), _const_spec(sel.shape)],
        out_specs=v_spec,
        out_shape=jax.ShapeDtypeStruct((bsz, seq, W_VB), BF16),
        scratch_shapes=[
            pltpu.VMEM((seq, GLA_DK), F32),
            pltpu.VMEM((seq, GLA_DK), BF16),
            pltpu.VMEM((seq, GLA_DK), BF16),
            pltpu.VMEM((seq // GLA_CHUNK, 8, GLA_DK), F32),
            pltpu.VMEM((256, GLA_SUB * GLA_DK), BF16),
            pltpu.VMEM((seq, LANES), F32),
            pltpu.VMEM((GLA_DV, GLA_DK), F32),
        ],
        compiler_params=pltpu.CompilerParams(dimension_semantics=("arbitrary", "arbitrary"),
                                             vmem_limit_bytes=VMEM_LIMIT_BYTES),
        name="gla",
    )(qb, kb, vb, la, rb, gn, sel)


def _merge_ffn_kernel(x_ref, oa_ref, ob_ref, ga_ref, gb_ref, pa_ref, pb_ref, wo_ref, g2_ref, wg_ref, wu_ref,
                      wd_ref, o_ref, acc_scr):
    y = (ga_ref[...].astype(F32) * _dot(oa_ref[...], pa_ref[...])
         + gb_ref[...].astype(F32) * _dot(ob_ref[...], pb_ref[...]))
    x1 = x_ref[...] + _dot(y.astype(BF16), wo_ref[...])
    h2 = _rms(x1, g2_ref[...]).astype(BF16)
    acc_scr[...] = x1
    n_blocks = 2
    fb = D_FF // n_blocks
    for j in range(n_blocks):
        gate = _dot(h2, wg_ref[:, j * fb:(j + 1) * fb])
        up = _dot(h2, wu_ref[:, j * fb:(j + 1) * fb])
        act = (gate * _sigmoid(gate) * up).astype(BF16)
        acc_scr[...] += _dot(act, wd_ref[j * fb:(j + 1) * fb, :])
    o_ref[...] = acc_scr[...]


def _merge_ffn(x2, oa, ob, ga, gb, w, g2, tm):
    n_tok = x2.shape[0]
    row = lambda width: pl.BlockSpec((tm, width), lambda i: (i, 0))
    weights = [w["pa"], w["pb"], w["wo"], g2, w["wg"], w["wu"], w["wd"]]
    return pl.pallas_call(
        _merge_ffn_kernel,
        grid=(n_tok // tm,),
        in_specs=[row(D_MODEL), row(W_KA), row(W_VB), row(D_MODEL), row(D_MODEL)]
        + [_const_spec(a.shape) for a in weights],
        out_specs=row(D_MODEL),
        out_shape=jax.ShapeDtypeStruct((n_tok, D_MODEL), F32),
        scratch_shapes=[pltpu.VMEM((tm, D_MODEL), F32)],
        compiler_params=pltpu.CompilerParams(dimension_semantics=("arbitrary",), vmem_limit_bytes=VMEM_LIMIT_BYTES),
        name="merge_ffn",
    )(x2, oa, ob, ga, gb, *weights)


def _layer_weights(w_in, w_a_up, b_a, w_proj_a, w_proj_b, w_out, w_ffn_gate, w_ffn_up, w_ffn_down):
    pieces = {}
    start = 0
    for name, width in zip(("qa", "ka", "va", "qb", "kb", "vb", "rb", "ab", "ga", "gb"), IN_SPLITS):
        pieces[name] = w_in[:, start:start + width].astype(BF16)
        start += width
    pieces["ab"] = jnp.pad(pieces["ab"], ((0, 0), (0, LANES - GLA_LOWRANK)))
    pieces["au"] = jnp.pad(w_a_up.astype(BF16), ((0, LANES - GLA_LOWRANK), (0, 0)))
    pieces["ba"] = b_a[None, :]
    pieces["pa"] = w_proj_a.astype(BF16)
    pieces["pb"] = w_proj_b.astype(BF16)
    pieces["wo"] = w_out.astype(BF16)
    pieces["wg"] = w_ffn_gate.astype(BF16)
    pieces["wu"] = w_ffn_up.astype(BF16)
    pieces["wd"] = w_ffn_down.astype(BF16)
    return pieces


def _layer(x2, bsz, seq, cos, sin, norm1, qn_a, kn_a, gn_b, norm2, w):
    qa, ka, va, qb, kb, vb, rb, la, ga, gb = _inproj(x2, cos, sin, norm1[None, :], qn_a[None, :], kn_a[None, :], w,
                                                     tm=256)
    shape3 = lambda a: a.reshape(bsz, seq, a.shape[-1])
    oa = _attention(shape3(qa), shape3(ka), shape3(va))
    ob = _gla(shape3(qb), shape3(kb), shape3(vb), shape3(la), shape3(rb), gn_b[None, :])
    return _merge_ffn(x2, oa.reshape(bsz * seq, -1), ob.reshape(bsz * seq, -1), ga, gb, w, norm2[None, :], tm=256)


def kernel(x, positions, norm1, w_in, qn_a, kn_a, w_a_up, b_a, gn_b, w_proj_a, w_proj_b, w_out, norm2,
           w_ffn_gate, w_ffn_up, w_ffn_down):
    bsz, seq, _ = x.shape
    cos, sin = _rope_tables(positions)
    x2 = x.reshape(bsz * seq, D_MODEL)
    for l in range(w_in.shape[0]):
        w = _layer_weights(w_in[l], w_a_up[l], b_a[l], w_proj_a[l], w_proj_b[l], w_out[l], w_ffn_gate[l],
                           w_ffn_up[l], w_ffn_down[l])
        x2 = _layer(x2, bsz, seq, cos, sin, norm1[l], qn_a[l], kn_a[l], gn_b[l], norm2[l], w)
    return x2.reshape(bsz, seq, D_MODEL)
```

```python
import jax
import jax.numpy as jnp
from jax import lax
from jax.experimental import pallas as pl
from jax.experimental.pallas import tpu as pltpu

F32 = jnp.float32
BF16 = jnp.bfloat16

D_MODEL = 1024
HEAD_DIM = 128
KV_HEADS = 4
DILATIONS = (1, 4, 16)
KEYS_PER_BLOCK = 128
Q_HEADS = KV_HEADS * len(DILATIONS)
ROPE_THETA = 10000.0
GLA_HEADS = 4
GLA_DK = 128
GLA_DV = 256
GLA_LOWRANK = 16
GLA_GATE_TEMP = 16.0
GLA_SUB = 8
GLA_STEP_ROWS = 256
D_FF = 2816
EPS = 1e-6
LOG2_E = 1.4426950408889634

W_QA = Q_HEADS * HEAD_DIM
W_KA = KV_HEADS * HEAD_DIM
W_QB = GLA_HEADS * GLA_DK
W_VB = GLA_HEADS * GLA_DV
IN_SPLITS = (W_QA, W_KA, W_KA, W_QB, W_QB, W_VB, W_VB, GLA_LOWRANK, D_MODEL, D_MODEL)

LANES = 128
VMEM_LIMIT_BYTES = 56 * 1024 * 1024


def _dot(a, b):
    return jnp.dot(a, b, preferred_element_type=F32)


def _dot_nt(a, b):
    return lax.dot_general(a, b, (((1,), (1,)), ((), ())), preferred_element_type=F32)


def _dot_tn(a, b):
    return lax.dot_general(a, b, (((0,), (0,)), ((), ())), preferred_element_type=F32)


def _rms(t, gain):
    return t * lax.rsqrt(jnp.mean(t * t, axis=-1, keepdims=True) + EPS) * gain


def _sigmoid(t):
    return 1.0 / (1.0 + jnp.exp(-t))


def _const_spec(shape):
    return pl.BlockSpec(shape, lambda *_: (0,) * len(shape))


def _rope_kernel(pos_ref, invf_ref, sign_ref, cos_ref, sin_ref):
    ang = pos_ref[...].astype(F32) * invf_ref[...]
    cos_ref[...] = jnp.cos(ang)
    sin_ref[...] = jnp.sin(ang) * sign_ref[...]


def _rope_tables(positions):
    n_tok = positions.size
    inv_freq = ROPE_THETA ** (-jnp.arange(0, HEAD_DIM, 2, dtype=F32) / HEAD_DIM)
    invf = jnp.concatenate([inv_freq, inv_freq])[None, :]
    sign = jnp.concatenate([-jnp.ones((HEAD_DIM // 2,), F32), jnp.ones((HEAD_DIM // 2,), F32)])[None, :]
    tm = 2048
    return pl.pallas_call(
        _rope_kernel,
        grid=(n_tok // tm,),
        in_specs=[pl.BlockSpec((tm, 1), lambda i: (i, 0)), _const_spec((1, HEAD_DIM)), _const_spec((1, HEAD_DIM))],
        out_specs=[pl.BlockSpec((tm, HEAD_DIM), lambda i: (i, 0))] * 2,
        out_shape=[jax.ShapeDtypeStruct((n_tok, HEAD_DIM), F32)] * 2,
        name="rope_tables",
    )(positions.reshape(n_tok, 1), invf, sign)


def _inproj_kernel(x_ref, g1_ref, cos_ref, sin_ref, qn_ref, kn_ref, wqa_ref, wka_ref, wva_ref, wqb_ref,
                   wkb_ref, wvb_ref, wrb_ref, wab_ref, wau_ref, ba_ref, wga_ref, wgb_ref,
                   qa_o, ka_o, va_o, qb_o, kb_o, vb_o, rb_o, la_o, ga_o, gb_o):
    h = _rms(x_ref[...], g1_ref[...]).astype(BF16)
    cos = cos_ref[...]
    sin = sin_ref[...]

    def norm_rope(t, gain, scale):
        t = _rms(t, gain)
        t = t * cos + pltpu.roll(t, HEAD_DIM // 2, 1) * sin
        return t * scale

    def per_head(w_ref, out_ref, gain, scale):
        width = w_ref.shape[1]
        step = 4 * HEAD_DIM
        for c0 in range(0, width, step):
            t = _dot(h, w_ref[:, c0:c0 + step])
            for j in range(0, step, HEAD_DIM):
                out_ref[:, c0 + j:c0 + j + HEAD_DIM] = norm_rope(t[:, j:j + HEAD_DIM], gain, scale)

    per_head(wqa_ref, qa_o, qn_ref[...], HEAD_DIM ** -0.5)
    per_head(wka_ref, ka_o, kn_ref[...], 1.0)
    va_o[...] = _dot(h, wva_ref[...])
    qb_o[...] = (_dot(h, wqb_ref[...]) * (GLA_DK ** -0.5)).astype(BF16)
    kb_o[...] = _dot(h, wkb_ref[...]).astype(BF16)
    for c0 in range(0, W_VB, 512):
        vb_o[:, c0:c0 + 512] = _dot(h, wvb_ref[:, c0:c0 + 512]).astype(BF16)
        r = _dot(h, wrb_ref[:, c0:c0 + 512])
        rb_o[:, c0:c0 + 512] = (r * _sigmoid(r)).astype(BF16)
        ga_o[:, c0:c0 + 512] = _sigmoid(_dot(h, wga_ref[:, c0:c0 + 512])).astype(BF16)
        gb_o[:, c0:c0 + 512] = _sigmoid(_dot(h, wgb_ref[:, c0:c0 + 512])).astype(BF16)
    ab = _dot(h, wab_ref[...]).astype(BF16)
    z = _dot(ab, wau_ref[...]) + ba_ref[...]
    log_sig = jnp.minimum(z, 0.0) - jnp.log1p(jnp.exp(-jnp.abs(z)))
    la_o[...] = log_sig * (LOG2_E / GLA_GATE_TEMP)


def _inproj(x2, cos, sin, g1, qn, kn, w, tm):
    n_tok = x2.shape[0]
    row = lambda width: pl.BlockSpec((tm, width), lambda i: (i, 0))
    weights = [w["qa"], w["ka"], w["va"], w["qb"], w["kb"], w["vb"], w["rb"], w["ab"], w["au"], w["ba"], w["ga"], w["gb"]]
    in_specs = ([row(D_MODEL), _const_spec((1, D_MODEL)), row(HEAD_DIM), row(HEAD_DIM),
                 _const_spec((1, HEAD_DIM)), _const_spec((1, HEAD_DIM))]
                + [_const_spec(a.shape) for a in weights])
    outs = [(W_QA, F32), (W_KA, F32), (W_KA, F32), (W_QB, BF16), (W_QB, BF16), (W_VB, BF16), (W_VB, BF16),
            (W_QB, F32), (D_MODEL, BF16), (D_MODEL, BF16)]
    return pl.pallas_call(
        _inproj_kernel,
        grid=(n_tok // tm,),
        in_specs=in_specs,
        out_specs=[row(width) for width, _ in outs],
        out_shape=[jax.ShapeDtypeStruct((n_tok, width), dt) for width, dt in outs],
        compiler_params=pltpu.CompilerParams(dimension_semantics=("arbitrary",), vmem_limit_bytes=VMEM_LIMIT_BYTES),
        name="inproj",
    )(x2, g1, cos, sin, qn, kn, *weights)


ATTN_BLOCKS_PER_STEP = 8


def _largest_divisor(n, limit):
    return max(u for u in range(1, limit + 1) if n % u == 0)


def _attn_kernel(q0_ref, q1_ref, q2_ref, k_ref, v_ref, o_ref, out_scr, lse_scr):
    seq = k_ref.shape[0]
    nk = KEYS_PER_BLOCK
    qi = lax.broadcasted_iota(jnp.int32, (nk, 2 * nk), 0) + nk
    ki = lax.broadcasted_iota(jnp.int32, (nk, 2 * nk), 1)
    dist = qi - ki
    band_mask = (dist >= 0) & (dist <= nk)
    qc = lax.broadcasted_iota(jnp.int32, (nk, nk), 0)
    kc = lax.broadcasted_iota(jnp.int32, (nk, nk), 1)
    causal_mask = qc >= kc

    def rows(start, size, stride):
        return pl.ds(start, size) if stride == 1 else pl.ds(start, size, stride=stride)

    q_refs = (q0_ref, q1_ref, q2_ref)

    def blocks(todo, n_keys, mask):
        strides = [DILATIONS[g] for g, _, _ in todo]
        q = jnp.stack([q_refs[g][rows(qs, nk, d), :] for (g, qs, _), d in zip(todo, strides)]).astype(BF16)
        k = jnp.stack([k_ref[rows(ks, n_keys, d), :] for (_, _, ks), d in zip(todo, strides)]).astype(BF16)
        v = jnp.stack([v_ref[rows(ks, n_keys, d), :] for (_, _, ks), d in zip(todo, strides)]).astype(BF16)
        s = lax.dot_general(q, k, (((2,), (2,)), ((0,), (0,))), preferred_element_type=F32)
        s = jnp.where(mask[None], s, -jnp.inf)
        m = jnp.max(s, axis=-1, keepdims=True)
        p = jnp.exp(s - m)
        den = jnp.sum(p, axis=-1, keepdims=True)
        o = lax.dot_general(p.astype(BF16), v, (((2,), (1,)), ((0,), (0,))), preferred_element_type=F32) / den
        lse = jnp.broadcast_to(m + jnp.log(den), o.shape)
        for u, ((g, qs, _), d) in enumerate(zip(todo, strides)):
            out_scr[g, rows(qs, nk, d), :] = o[u]
            lse_scr[g, rows(qs, nk, d), :] = lse[u]

    n_groups = len(DILATIONS)
    small = [(g, r, r) for g in range(n_groups - 1) for r in range(DILATIONS[g])]
    blocks(small, nk, causal_mask)
    last = n_groups - 1
    per_step = _largest_divisor(DILATIONS[last], ATTN_BLOCKS_PER_STEP)

    def first_step(i, carry):
        blocks([(last, i * per_step + u, i * per_step + u) for u in range(per_step)], nk, causal_mask)
        return carry

    lax.fori_loop(0, DILATIONS[last] // per_step, first_step, 0)
    for g, stride in enumerate(DILATIONS):
        span = nk * stride
        n_later = seq // span - 1
        if n_later <= 0:
            continue
        n_total = stride * n_later
        per_step = _largest_divisor(n_total, ATTN_BLOCKS_PER_STEP)

        def later_step(i, carry, g=g, span=span, n_later=n_later, per_step=per_step):
            todo = []
            for u in range(per_step):
                j = i * per_step + u
                r = j // n_later
                n = j % n_later + 1
                todo.append((g, r + span * n, r + span * (n - 1)))
            blocks(todo, 2 * nk, band_mask)
            return carry

        lax.fori_loop(0, n_total // per_step, later_step, 0)

    chunk = 256

    def combine(i, carry):
        r0 = pl.multiple_of(i * chunk, chunk)
        l0 = lse_scr[0, pl.ds(r0, chunk), :]
        l1 = lse_scr[1, pl.ds(r0, chunk), :]
        l2 = lse_scr[2, pl.ds(r0, chunk), :]
        mx = jnp.maximum(jnp.maximum(l0, l1), l2)
        w0 = jnp.exp(l0 - mx)
        w1 = jnp.exp(l1 - mx)
        w2 = jnp.exp(l2 - mx)
        num = (w0 * out_scr[0, pl.ds(r0, chunk), :] + w1 * out_scr[1, pl.ds(r0, chunk), :]
               + w2 * out_scr[2, pl.ds(r0, chunk), :])
        o_ref[pl.ds(r0, chunk), :] = (num / (w0 + w1 + w2)).astype(o_ref.dtype)
        return carry

    lax.fori_loop(0, seq // chunk, combine, 0)


def _attention(qa, ka, va):
    bsz, seq, _ = ka.shape
    n_groups = len(DILATIONS)

    def q_spec(group):
        return pl.BlockSpec((None, seq, HEAD_DIM), lambda b, h: (b, 0, group * KV_HEADS + h))

    kv_spec = pl.BlockSpec((None, seq, HEAD_DIM), lambda b, h: (b, 0, h))
    return pl.pallas_call(
        _attn_kernel,
        grid=(bsz, KV_HEADS),
        in_specs=[q_spec(0), q_spec(1), q_spec(2), kv_spec, kv_spec],
        out_specs=kv_spec,
        out_shape=jax.ShapeDtypeStruct((bsz, seq, KV_HEADS * HEAD_DIM), BF16),
        scratch_shapes=[pltpu.VMEM((n_groups, seq, HEAD_DIM), F32), pltpu.VMEM((n_groups, seq, HEAD_DIM), F32)],
        compiler_params=pltpu.CompilerParams(dimension_semantics=("arbitrary", "arbitrary"),
                                             vmem_limit_bytes=VMEM_LIMIT_BYTES),
        name="dilated_attention",
    )(qa, qa, qa, ka, va)


GLA_HALVES = tuple(h for h in (8, 16, 32, 64, 128, 256, 512) if GLA_SUB <= h < GLA_STEP_ROWS)


def _gla_kernel(q_ref, k_ref, v_ref, la_ref, rb_ref, gn_ref, sel_ref, tri_ref, mask_ref, o_ref, x_scr, st_scr):
    seq = q_ref.shape[0]
    rows = GLA_STEP_ROWS
    n_subs = rows // GLA_SUB
    st_scr[...] = jnp.zeros_like(st_scr)
    sub_row = lax.broadcasted_iota(jnp.int32, (n_subs, GLA_SUB, GLA_DK), 1)
    gn = gn_ref[...]

    def step(i, carry):
        r0 = pl.multiple_of(i * rows, rows)
        la = la_ref[pl.ds(r0, rows), :]
        hi = la.astype(BF16)
        rest = la - hi.astype(F32)
        mid = rest.astype(BF16)
        low = (rest - mid.astype(F32)).astype(BF16)
        tri = tri_ref[...]
        b = _dot(tri, hi) + _dot(tri, mid) + _dot(tri, low)
        q = q_ref[pl.ds(r0, rows), :].astype(F32)
        k = k_ref[pl.ds(r0, rows), :].astype(F32)
        v = v_ref[pl.ds(r0, rows), :]
        b_last = b[rows - 1:rows, :]
        qh = (q * jnp.exp2(b)).astype(BF16)
        kh = (k * jnp.exp2(b_last - b)).astype(BF16)
        qs = q.reshape(n_subs, GLA_SUB, GLA_DK)
        ks = k.reshape(n_subs, GLA_SUB, GLA_DK)
        bs = b.reshape(n_subs, GLA_SUB, GLA_DK)
        for j in range(GLA_SUB):
            decay = jnp.exp2(jnp.where(sub_row >= j, bs - bs[:, j:j + 1, :], -jnp.inf))
            xj = qs * ks[:, j:j + 1, :] * decay
            x_scr[:, j * GLA_DK:(j + 1) * GLA_DK] = xj.reshape(rows, GLA_DK).astype(BF16)
        att = mask_ref[0] * _dot(x_scr[...], sel_ref[...])
        for level, h in enumerate(GLA_HALVES):
            shape = (rows // (2 * h), 2 * h, GLA_DK)
            bq, q3, k3 = b.reshape(shape), q.reshape(shape), k.reshape(shape)
            b_mid = bq[:, h - 1:h, :]
            q_up = q3[:, h:, :] * jnp.exp2(bq[:, h:, :] - b_mid)
            k_lo = k3[:, :h, :] * jnp.exp2(b_mid - bq[:, :h, :])
            q_l = jnp.concatenate([q3[:, :h, :], q_up], axis=1).reshape(rows, GLA_DK).astype(BF16)
            k_l = jnp.concatenate([k_lo, k3[:, h:, :]], axis=1).reshape(rows, GLA_DK).astype(BF16)
            att = att + mask_ref[level + 1] * _dot_nt(q_l, k_l)
        st = st_scr[...]
        o = _dot_nt(qh, st.astype(BF16)) + _dot(att.astype(BF16), v)
        st_scr[...] = st * jnp.exp2(b_last) + _dot_tn(v, kh)
        y = _rms(o, gn) * rb_ref[pl.ds(r0, rows), :].astype(F32)
        o_ref[pl.ds(r0, rows), :] = y.astype(o_ref.dtype)
        return carry

    lax.fori_loop(0, seq // rows, step, 0, unroll=2)


def _gla_constants():
    rows = GLA_STEP_ROWS
    j = jnp.arange(GLA_SUB * GLA_DK)[:, None] // GLA_DK
    c = jnp.arange(rows)[None, :]
    sel = (c % GLA_SUB == j).astype(BF16)
    t = jnp.arange(rows)[:, None]
    s = jnp.arange(rows)[None, :]
    tri = (s <= t).astype(BF16)
    masks = [(t // GLA_SUB == s // GLA_SUB) & (s <= t)]
    for h in GLA_HALVES:
        masks.append((t // (2 * h) == s // (2 * h)) & (t % (2 * h) >= h) & (s % (2 * h) < h))
    return sel, tri, jnp.stack(masks).astype(F32)


def _gla(qb, kb, vb, la, rb, gn):
    bsz, seq, _ = qb.shape
    k_spec = pl.BlockSpec((None, seq, GLA_DK), lambda b, h: (b, 0, h))
    v_spec = pl.BlockSpec((None, seq, GLA_DV), lambda b, h: (b, 0, h))
    sel, tri, masks = _gla_constants()
    return pl.pallas_call(
        _gla_kernel,
        grid=(bsz, GLA_HEADS),
        in_specs=[k_spec, k_spec, v_spec, k_spec, v_spec, _const_spec((1, GLA_DV)), _const_spec(sel.shape),
                  _const_spec(tri.shape), _const_spec(masks.shape)],
        out_specs=v_spec,
        out_shape=jax.ShapeDtypeStruct((bsz, seq, W_VB), BF16),
        scratch_shapes=[
            pltpu.VMEM((GLA_STEP_ROWS, GLA_SUB * GLA_DK), BF16),
            pltpu.VMEM((GLA_DV, GLA_DK), F32),
        ],
        compiler_params=pltpu.CompilerParams(dimension_semantics=("arbitrary", "arbitrary"),
                                             vmem_limit_bytes=VMEM_LIMIT_BYTES),
        name="gla",
    )(qb, kb, vb, la, rb, gn, sel, tri, masks)


def _merge_ffn_kernel(x_ref, oa_ref, ob_ref, ga_ref, gb_ref, pa_ref, pb_ref, wo_ref, g2_ref, wg_ref, wu_ref,
                      wd_ref, o_ref, acc_scr):
    y = (ga_ref[...].astype(F32) * _dot(oa_ref[...], pa_ref[...])
         + gb_ref[...].astype(F32) * _dot(ob_ref[...], pb_ref[...]))
    x1 = x_ref[...] + _dot(y.astype(BF16), wo_ref[...])
    h2 = _rms(x1, g2_ref[...]).astype(BF16)
    acc_scr[...] = x1
    n_blocks = 2
    fb = D_FF // n_blocks
    for j in range(n_blocks):
        gate = _dot(h2, wg_ref[:, j * fb:(j + 1) * fb])
        up = _dot(h2, wu_ref[:, j * fb:(j + 1) * fb])
        act = (gate * _sigmoid(gate) * up).astype(BF16)
        acc_scr[...] += _dot(act, wd_ref[j * fb:(j + 1) * fb, :])
    o_ref[...] = acc_scr[...]


def _merge_ffn(x2, oa, ob, ga, gb, w, g2, tm):
    n_tok = x2.shape[0]
    row = lambda width: pl.BlockSpec((tm, width), lambda i: (i, 0))
    weights = [w["pa"], w["pb"], w["wo"], g2, w["wg"], w["wu"], w["wd"]]
    return pl.pallas_call(
        _merge_ffn_kernel,
        grid=(n_tok // tm,),
        in_specs=[row(D_MODEL), row(W_KA), row(W_VB), row(D_MODEL), row(D_MODEL)]
        + [_const_spec(a.shape) for a in weights],
        out_specs=row(D_MODEL),
        out_shape=jax.ShapeDtypeStruct((n_tok, D_MODEL), F32),
        scratch_shapes=[pltpu.VMEM((tm, D_MODEL), F32)],
        compiler_params=pltpu.CompilerParams(dimension_semantics=("arbitrary",), vmem_limit_bytes=VMEM_LIMIT_BYTES),
        name="merge_ffn",
    )(x2, oa, ob, ga, gb, *weights)


def _layer_weights(w_in, w_a_up, b_a, w_proj_a, w_proj_b, w_out, w_ffn_gate, w_ffn_up, w_ffn_down):
    pieces = {}
    start = 0
    for name, width in zip(("qa", "ka", "va", "qb", "kb", "vb", "rb", "ab", "ga", "gb"), IN_SPLITS):
        pieces[name] = w_in[:, start:start + width].astype(BF16)
        start += width
    pieces["ab"] = jnp.pad(pieces["ab"], ((0, 0), (0, LANES - GLA_LOWRANK)))
    pieces["au"] = jnp.pad(w_a_up.astype(BF16), ((0, LANES - GLA_LOWRANK), (0, 0)))
    pieces["ba"] = b_a[None, :]
    pieces["pa"] = w_proj_a.astype(BF16)
    pieces["pb"] = w_proj_b.astype(BF16)
    pieces["wo"] = w_out.astype(BF16)
    pieces["wg"] = w_ffn_gate.astype(BF16)
    pieces["wu"] = w_ffn_up.astype(BF16)
    pieces["wd"] = w_ffn_down.astype(BF16)
    return pieces


def _layer(x2, bsz, seq, cos, sin, norm1, qn_a, kn_a, gn_b, norm2, w):
    qa, ka, va, qb, kb, vb, rb, la, ga, gb = _inproj(x2, cos, sin, norm1[None, :], qn_a[None, :], kn_a[None, :], w,
                                                     tm=256)
    shape3 = lambda a: a.reshape(bsz, seq, a.shape[-1])
    oa = _attention(shape3(qa), shape3(ka), shape3(va))
    ob = _gla(shape3(qb), shape3(kb), shape3(vb), shape3(la), shape3(rb), gn_b[None, :])
    return _merge_ffn(x2, oa.reshape(bsz * seq, -1), ob.reshape(bsz * seq, -1), ga, gb, w, norm2[None, :], tm=256)


def kernel(x, positions, norm1, w_in, qn_a, kn_a, w_a_up, b_a, gn_b, w_proj_a, w_proj_b, w_out, norm2,
           w_ffn_gate, w_ffn_up, w_ffn_down):
    bsz, seq, _ = x.shape
    cos, sin = _rope_tables(positions)
    x2 = x.reshape(bsz * seq, D_MODEL)
    for l in range(w_in.shape[0]):
        w = _layer_weights(w_in[l], w_a_up[l], b_a[l], w_proj_a[l], w_proj_b[l], w_out[l], w_ffn_gate[l],
                           w_ffn_up[l], w_ffn_down[l])
        x2 = _layer(x2, bsz, seq, cos, sin, norm1[l], qn_a[l], kn_a[l], gn_b[l], norm2[l], w)
    return x2.reshape(bsz, seq, D_MODEL)
```

```python
import jax
import jax.numpy as jnp
from jax import lax
from jax.experimental import pallas as pl
from jax.experimental.pallas import tpu as pltpu

F32 = jnp.float32
BF16 = jnp.bfloat16

D_MODEL = 1024
HEAD_DIM = 128
KV_HEADS = 4
DILATIONS = (1, 4, 16)
KEYS_PER_BLOCK = 128
Q_HEADS = KV_HEADS * len(DILATIONS)
ROPE_THETA = 10000.0
GLA_HEADS = 4
GLA_DK = 128
GLA_DV = 256
GLA_LOWRANK = 16
GLA_GATE_TEMP = 16.0
GLA_SUB = 8
GLA_STEP_ROWS = 256
D_FF = 2816
EPS = 1e-6
LOG2_E = 1.4426950408889634

W_QA = Q_HEADS * HEAD_DIM
W_KA = KV_HEADS * HEAD_DIM
W_QB = GLA_HEADS * GLA_DK
W_VB = GLA_HEADS * GLA_DV
IN_SPLITS = (W_QA, W_KA, W_KA, W_QB, W_QB, W_VB, W_VB, GLA_LOWRANK, D_MODEL, D_MODEL)

LANES = 128
MXU_TILE = 256
FFN_BLOCKS = ((0, 6 * MXU_TILE), (6 * MXU_TILE, D_FF))
TOKEN_TILE = 512
VMEM_LIMIT_BYTES = 56 * 1024 * 1024


def _dot(a, b):
    return jnp.dot(a, b, preferred_element_type=F32)


def _dot_nt(a, b):
    return lax.dot_general(a, b, (((1,), (1,)), ((), ())), preferred_element_type=F32)


def _dot_tn(a, b):
    return lax.dot_general(a, b, (((0,), (0,)), ((), ())), preferred_element_type=F32)


def _rms(t, gain):
    return t * lax.rsqrt(jnp.mean(t * t, axis=-1, keepdims=True) + EPS) * gain


def _sigmoid(t):
    return 1.0 / (1.0 + jnp.exp(-t))


def _const_spec(shape):
    return pl.BlockSpec(shape, lambda *_: (0,) * len(shape), pipeline_mode=pl.Buffered(1))


def _rope_kernel(pos_ref, invf_ref, sign_ref, cos_ref, sin_ref):
    ang = pos_ref[...].astype(F32) * invf_ref[...]
    cos_ref[...] = jnp.cos(ang)
    sin_ref[...] = jnp.sin(ang) * sign_ref[...]


def _rope_tables(positions):
    n_tok = positions.size
    inv_freq = ROPE_THETA ** (-jnp.arange(0, HEAD_DIM, 2, dtype=F32) / HEAD_DIM)
    invf = jnp.concatenate([inv_freq, inv_freq])[None, :]
    sign = jnp.concatenate([-jnp.ones((HEAD_DIM // 2,), F32), jnp.ones((HEAD_DIM // 2,), F32)])[None, :]
    tm = 2048
    return pl.pallas_call(
        _rope_kernel,
        grid=(n_tok // tm,),
        in_specs=[pl.BlockSpec((tm, 1), lambda i: (i, 0)), _const_spec((1, HEAD_DIM)), _const_spec((1, HEAD_DIM))],
        out_specs=[pl.BlockSpec((tm, HEAD_DIM), lambda i: (i, 0))] * 2,
        out_shape=[jax.ShapeDtypeStruct((n_tok, HEAD_DIM), F32)] * 2,
        name="rope_tables",
    )(positions.reshape(n_tok, 1), invf, sign)


def _inproj_kernel(x_ref, g1_ref, cos_ref, sin_ref, qn_ref, kn_ref, wqa_ref, wka_ref, wva_ref, wqb_ref,
                   wkb_ref, wvb_ref, wrb_ref, wab_ref, wau_ref, ba_ref, wga_ref, wgb_ref,
                   qa_o, ka_o, va_o, qb_o, kb_o, vb_o, rb_o, la_o, ga_o, gb_o):
    h = _rms(x_ref[...], g1_ref[...]).astype(BF16)
    cos = cos_ref[...]
    sin = sin_ref[...]

    def norm_rope(t, gain, scale):
        t = _rms(t, gain)
        t = t * cos + pltpu.roll(t, HEAD_DIM // 2, 1) * sin
        return t * scale

    def per_head(w_ref, out_ref, gain, scale):
        width = w_ref.shape[1]
        step = 4 * HEAD_DIM
        for c0 in range(0, width, step):
            t = _dot(h, w_ref[:, c0:c0 + step])
            for j in range(0, step, HEAD_DIM):
                out_ref[:, c0 + j:c0 + j + HEAD_DIM] = norm_rope(t[:, j:j + HEAD_DIM], gain, scale)

    per_head(wqa_ref, qa_o, qn_ref[...], HEAD_DIM ** -0.5 * LOG2_E)
    per_head(wka_ref, ka_o, kn_ref[...], 1.0)
    va_o[...] = _dot(h, wva_ref[...])
    qb_o[...] = (_dot(h, wqb_ref[...]) * (GLA_DK ** -0.5)).astype(BF16)
    kb_o[...] = _dot(h, wkb_ref[...]).astype(BF16)
    for c0 in range(0, W_VB, 512):
        vb_o[:, c0:c0 + 512] = _dot(h, wvb_ref[:, c0:c0 + 512]).astype(BF16)
        r = _dot(h, wrb_ref[:, c0:c0 + 512])
        rb_o[:, c0:c0 + 512] = (r * _sigmoid(r)).astype(BF16)
        ga_o[:, c0:c0 + 512] = _sigmoid(_dot(h, wga_ref[:, c0:c0 + 512])).astype(BF16)
        gb_o[:, c0:c0 + 512] = _sigmoid(_dot(h, wgb_ref[:, c0:c0 + 512])).astype(BF16)
    ab = _dot(h, wab_ref[...]).astype(BF16)
    z = _dot(ab, wau_ref[...]) + ba_ref[...]
    log_sig = jnp.minimum(z, 0.0) - jnp.log1p(jnp.exp(-jnp.abs(z)))
    la_o[...] = log_sig * (LOG2_E / GLA_GATE_TEMP)


def _inproj(x2, cos, sin, g1, qn, kn, w, tm):
    n_tok = x2.shape[0]
    row = lambda width: pl.BlockSpec((tm, width), lambda i: (i, 0))
    weights = [w["qa"], w["ka"], w["va"], w["qb"], w["kb"], w["vb"], w["rb"], w["ab"], w["au"], w["ba"], w["ga"], w["gb"]]
    in_specs = ([row(D_MODEL), _const_spec((1, D_MODEL)), row(HEAD_DIM), row(HEAD_DIM),
                 _const_spec((1, HEAD_DIM)), _const_spec((1, HEAD_DIM))]
                + [_const_spec(a.shape) for a in weights])
    outs = [(W_QA, F32), (W_KA, F32), (W_KA, F32), (W_QB, BF16), (W_QB, BF16), (W_VB, BF16), (W_VB, BF16),
            (W_QB, F32), (D_MODEL, BF16), (D_MODEL, BF16)]
    return pl.pallas_call(
        _inproj_kernel,
        grid=(n_tok // tm,),
        in_specs=in_specs,
        out_specs=[row(width) for width, _ in outs],
        out_shape=[jax.ShapeDtypeStruct((n_tok, width), dt) for width, dt in outs],
        compiler_params=pltpu.CompilerParams(dimension_semantics=("arbitrary",), vmem_limit_bytes=VMEM_LIMIT_BYTES),
        name="inproj",
    )(x2, g1, cos, sin, qn, kn, *weights)


ATTN_BLOCKS_PER_BATCH = 16


def _attn_kernel(q0_ref, q1_ref, q2_ref, k_ref, v_ref, o_ref, out_scr, lse_scr):
    seq = k_ref.shape[0]
    nk = KEYS_PER_BLOCK
    qi = lax.broadcasted_iota(jnp.int32, (nk, 2 * nk), 0) + nk
    ki = lax.broadcasted_iota(jnp.int32, (nk, 2 * nk), 1)
    dist = qi - ki
    band_mask = (dist >= 0) & (dist <= nk)
    qc = lax.broadcasted_iota(jnp.int32, (nk, nk), 0)
    kc = lax.broadcasted_iota(jnp.int32, (nk, nk), 1)
    causal_mask = qc >= kc

    def rows(start, size, stride):
        return pl.ds(start, size) if stride == 1 else pl.ds(start, size, stride=stride)

    q_refs = (q0_ref, q1_ref, q2_ref)

    def blocks(todo, n_keys, mask):
        strides = [DILATIONS[g] for g, _, _ in todo]
        q = jnp.stack([q_refs[g][rows(qs, nk, d), :] for (g, qs, _), d in zip(todo, strides)]).astype(BF16)
        k = jnp.stack([k_ref[rows(ks, n_keys, d), :] for (_, _, ks), d in zip(todo, strides)]).astype(BF16)
        v = jnp.stack([v_ref[rows(ks, n_keys, d), :] for (_, _, ks), d in zip(todo, strides)]).astype(BF16)
        s = lax.dot_general(q, k, (((2,), (2,)), ((0,), (0,))), preferred_element_type=F32)
        s = jnp.where(mask[None], s, -jnp.inf)
        m = jnp.max(s, axis=-1, keepdims=True)
        p = jnp.exp2(s - m)
        den = jnp.sum(p, axis=-1, keepdims=True)
        o = lax.dot_general(p.astype(BF16), v, (((2,), (1,)), ((0,), (0,))), preferred_element_type=F32) / den
        lse = jnp.broadcast_to(m + jnp.log2(den), o.shape)
        for u, ((g, qs, _), d) in enumerate(zip(todo, strides)):
            out_scr[g, rows(qs, nk, d), :] = o[u]
            lse_scr[g, rows(qs, nk, d), :] = lse[u]

    first, later = [], []
    for g, stride in enumerate(DILATIONS):
        span = nk * stride
        for r in range(stride):
            first.append((g, r, r))
            later += [(g, r + span * n, r + span * (n - 1)) for n in range(1, seq // span)]
    for todo, n_keys, mask in ((first, nk, causal_mask), (later, 2 * nk, band_mask)):
        n_batches = -(-len(todo) // ATTN_BLOCKS_PER_BATCH)
        size = -(-len(todo) // n_batches)
        for start in range(0, len(todo), size):
            blocks(todo[start:start + size], n_keys, mask)

    chunk = 256

    def combine(i, carry):
        r0 = pl.multiple_of(i * chunk, chunk)
        l0 = lse_scr[0, pl.ds(r0, chunk), :]
        l1 = lse_scr[1, pl.ds(r0, chunk), :]
        l2 = lse_scr[2, pl.ds(r0, chunk), :]
        mx = jnp.maximum(jnp.maximum(l0, l1), l2)
        w0 = jnp.exp2(l0 - mx)
        w1 = jnp.exp2(l1 - mx)
        w2 = jnp.exp2(l2 - mx)
        num = (w0 * out_scr[0, pl.ds(r0, chunk), :] + w1 * out_scr[1, pl.ds(r0, chunk), :]
               + w2 * out_scr[2, pl.ds(r0, chunk), :])
        o_ref[pl.ds(r0, chunk), :] = (num / (w0 + w1 + w2)).astype(o_ref.dtype)
        return carry

    lax.fori_loop(0, seq // chunk, combine, 0)


def _attention(qa, ka, va):
    bsz, seq, _ = ka.shape
    n_groups = len(DILATIONS)

    def q_spec(group):
        return pl.BlockSpec((None, seq, HEAD_DIM), lambda b, h: (b, 0, group * KV_HEADS + h))

    kv_spec = pl.BlockSpec((None, seq, HEAD_DIM), lambda b, h: (b, 0, h))
    return pl.pallas_call(
        _attn_kernel,
        grid=(bsz, KV_HEADS),
        in_specs=[q_spec(0), q_spec(1), q_spec(2), kv_spec, kv_spec],
        out_specs=kv_spec,
        out_shape=jax.ShapeDtypeStruct((bsz, seq, KV_HEADS * HEAD_DIM), BF16),
        scratch_shapes=[pltpu.VMEM((n_groups, seq, HEAD_DIM), F32), pltpu.VMEM((n_groups, seq, HEAD_DIM), F32)],
        compiler_params=pltpu.CompilerParams(dimension_semantics=("arbitrary", "arbitrary"),
                                             vmem_limit_bytes=VMEM_LIMIT_BYTES),
        name="dilated_attention",
    )(qa, qa, qa, ka, va)


GLA_HALVES = tuple(h for h in (8, 16, 32, 64, 128, 256, 512) if GLA_SUB <= h < GLA_STEP_ROWS)


def _gla_kernel(q_ref, k_ref, v_ref, la_ref, rb_ref, gn_ref, sel_ref, tri_ref, mask_ref, o_ref, x_scr, st_scr):
    seq = q_ref.shape[0]
    rows = GLA_STEP_ROWS
    n_subs = rows // GLA_SUB
    st_scr[...] = jnp.zeros_like(st_scr)
    sub_row = lax.broadcasted_iota(jnp.int32, (n_subs, GLA_SUB, GLA_DK), 1)
    gn = gn_ref[...]

    def step(i, carry):
        r0 = pl.multiple_of(i * rows, rows)
        la = la_ref[pl.ds(r0, rows), :]
        hi = la.astype(BF16)
        rest = la - hi.astype(F32)
        mid = rest.astype(BF16)
        low = (rest - mid.astype(F32)).astype(BF16)
        cum = _dot(tri_ref[...], jnp.concatenate([hi, mid, low], axis=1))
        b = cum[:, :GLA_DK] + cum[:, GLA_DK:2 * GLA_DK] + cum[:, 2 * GLA_DK:]
        q = q_ref[pl.ds(r0, rows), :].astype(F32)
        k = k_ref[pl.ds(r0, rows), :].astype(F32)
        v = v_ref[pl.ds(r0, rows), :]
        b_last = b[rows - 1:rows, :]
        qh = (q * jnp.exp2(b)).astype(BF16)
        kh = (k * jnp.exp2(b_last - b)).astype(BF16)
        qs = q.reshape(n_subs, GLA_SUB, GLA_DK)
        ks = k.reshape(n_subs, GLA_SUB, GLA_DK)
        bs = b.reshape(n_subs, GLA_SUB, GLA_DK)
        for j in range(GLA_SUB):
            decay = jnp.exp2(jnp.where(sub_row >= j, bs - bs[:, j:j + 1, :], -jnp.inf))
            xj = qs * ks[:, j:j + 1, :] * decay
            x_scr[:, j * GLA_DK:(j + 1) * GLA_DK] = xj.reshape(rows, GLA_DK).astype(BF16)
        att = mask_ref[0] * _dot(x_scr[...], sel_ref[...])
        for level, h in enumerate(GLA_HALVES):
            shape = (rows // (2 * h), 2 * h, GLA_DK)
            bq, q3, k3 = b.reshape(shape), q.reshape(shape), k.reshape(shape)
            b_mid = bq[:, h - 1:h, :]
            q_up = q3[:, h:, :] * jnp.exp2(bq[:, h:, :] - b_mid)
            k_lo = k3[:, :h, :] * jnp.exp2(b_mid - bq[:, :h, :])
            q_l = jnp.concatenate([q3[:, :h, :], q_up], axis=1).reshape(rows, GLA_DK).astype(BF16)
            k_l = jnp.concatenate([k_lo, k3[:, h:, :]], axis=1).reshape(rows, GLA_DK).astype(BF16)
            att = att + mask_ref[level + 1] * _dot_nt(q_l, k_l)
        st = st_scr[...]
        o = _dot_nt(qh, st.astype(BF16)) + _dot(att.astype(BF16), v)
        st_scr[...] = st * jnp.exp2(b_last) + _dot_tn(v, kh)
        y = _rms(o, gn) * rb_ref[pl.ds(r0, rows), :].astype(F32)
        o_ref[pl.ds(r0, rows), :] = y.astype(o_ref.dtype)
        return carry

    lax.fori_loop(0, seq // rows, step, 0, unroll=2)


def _gla_constants():
    rows = GLA_STEP_ROWS
    j = jnp.arange(GLA_SUB * GLA_DK)[:, None] // GLA_DK
    c = jnp.arange(rows)[None, :]
    sel = (c % GLA_SUB == j).astype(BF16)
    t = jnp.arange(rows)[:, None]
    s = jnp.arange(rows)[None, :]
    tri = (s <= t).astype(BF16)
    masks = [(t // GLA_SUB == s // GLA_SUB) & (s <= t)]
    for h in GLA_HALVES:
        masks.append((t // (2 * h) == s // (2 * h)) & (t % (2 * h) >= h) & (s % (2 * h) < h))
    return sel, tri, jnp.stack(masks).astype(F32)


def _gla(qb, kb, vb, la, rb, gn):
    bsz, seq, _ = qb.shape
    k_spec = pl.BlockSpec((None, seq, GLA_DK), lambda b, h: (b, 0, h))
    v_spec = pl.BlockSpec((None, seq, GLA_DV), lambda b, h: (b, 0, h))
    sel, tri, masks = _gla_constants()
    return pl.pallas_call(
        _gla_kernel,
        grid=(bsz, GLA_HEADS),
        in_specs=[k_spec, k_spec, v_spec, k_spec, v_spec, _const_spec((1, GLA_DV)), _const_spec(sel.shape),
                  _const_spec(tri.shape), _const_spec(masks.shape)],
        out_specs=v_spec,
        out_shape=jax.ShapeDtypeStruct((bsz, seq, W_VB), BF16),
        scratch_shapes=[
            pltpu.VMEM((GLA_STEP_ROWS, GLA_SUB * GLA_DK), BF16),
            pltpu.VMEM((GLA_DV, GLA_DK), F32),
        ],
        compiler_params=pltpu.CompilerParams(dimension_semantics=("arbitrary", "arbitrary"),
                                             vmem_limit_bytes=VMEM_LIMIT_BYTES),
        name="gla",
    )(qb, kb, vb, la, rb, gn, sel, tri, masks)


def _merge_ffn_kernel(x_ref, oa_ref, ob_ref, ga_ref, gb_ref, pa_ref, pb_ref, wo_ref, g2_ref, wg_ref, wu_ref,
                      wd_ref, o_ref, acc_scr):
    y = (ga_ref[...].astype(F32) * _dot(oa_ref[...], pa_ref[...])
         + gb_ref[...].astype(F32) * _dot(ob_ref[...], pb_ref[...]))
    x1 = x_ref[...] + _dot(y.astype(BF16), wo_ref[...])
    h2 = _rms(x1, g2_ref[...]).astype(BF16)
    acc_scr[...] = x1
    for lo, hi in FFN_BLOCKS:
        gate = _dot(h2, wg_ref[:, lo:hi])
        up = _dot(h2, wu_ref[:, lo:hi])
        act = (gate * _sigmoid(gate) * up).astype(BF16)
        acc_scr[...] += _dot(act, wd_ref[lo:hi, :])
    o_ref[...] = acc_scr[...]


def _merge_ffn(x2, oa, ob, ga, gb, w, g2, tm):
    n_tok = x2.shape[0]
    row = lambda width: pl.BlockSpec((tm, width), lambda i: (i, 0))
    weights = [w["pa"], w["pb"], w["wo"], g2, w["wg"], w["wu"], w["wd"]]
    return pl.pallas_call(
        _merge_ffn_kernel,
        grid=(n_tok // tm,),
        in_specs=[row(D_MODEL), row(W_KA), row(W_VB), row(D_MODEL), row(D_MODEL)]
        + [_const_spec(a.shape) for a in weights],
        out_specs=row(D_MODEL),
        out_shape=jax.ShapeDtypeStruct((n_tok, D_MODEL), F32),
        scratch_shapes=[pltpu.VMEM((tm, D_MODEL), F32)],
        compiler_params=pltpu.CompilerParams(dimension_semantics=("arbitrary",), vmem_limit_bytes=VMEM_LIMIT_BYTES),
        name="merge_ffn",
    )(x2, oa, ob, ga, gb, *weights)


def _layer_weights(w_in, w_a_up, b_a, w_proj_a, w_proj_b, w_out, w_ffn_gate, w_ffn_up, w_ffn_down):
    pieces = {}
    start = 0
    for name, width in zip(("qa", "ka", "va", "qb", "kb", "vb", "rb", "ab", "ga", "gb"), IN_SPLITS):
        pieces[name] = w_in[:, start:start + width].astype(BF16)
        start += width
    pieces["ab"] = jnp.pad(pieces["ab"], ((0, 0), (0, LANES - GLA_LOWRANK)))
    pieces["au"] = jnp.pad(w_a_up.astype(BF16), ((0, LANES - GLA_LOWRANK), (0, 0)))
    pieces["ba"] = b_a[None, :]
    pieces["pa"] = w_proj_a.astype(BF16)
    pieces["pb"] = w_proj_b.astype(BF16)
    pieces["wo"] = w_out.astype(BF16)
    pieces["wg"] = w_ffn_gate.astype(BF16)
    pieces["wu"] = w_ffn_up.astype(BF16)
    pieces["wd"] = w_ffn_down.astype(BF16)
    return pieces


def _layer(x2, bsz, seq, cos, sin, norm1, qn_a, kn_a, gn_b, norm2, w):
    qa, ka, va, qb, kb, vb, rb, la, ga, gb = _inproj(x2, cos, sin, norm1[None, :], qn_a[None, :], kn_a[None, :], w,
                                                     tm=TOKEN_TILE)
    shape3 = lambda a: a.reshape(bsz, seq, a.shape[-1])
    oa = _attention(shape3(qa), shape3(ka), shape3(va))
    ob = _gla(shape3(qb), shape3(kb), shape3(vb), shape3(la), shape3(rb), gn_b[None, :])
    return _merge_ffn(x2, oa.reshape(bsz * seq, -1), ob.reshape(bsz * seq, -1), ga, gb, w, norm2[None, :],
                      tm=TOKEN_TILE)


def kernel(x, positions, norm1, w_in, qn_a, kn_a, w_a_up, b_a, gn_b, w_proj_a, w_proj_b, w_out, norm2,
           w_ffn_gate, w_ffn_up, w_ffn_down):
    bsz, seq, _ = x.shape
    cos, sin = _rope_tables(positions)
    x2 = x.reshape(bsz * seq, D_MODEL)
    for l in range(w_in.shape[0]):
        w = _layer_weights(w_in[l], w_a_up[l], b_a[l], w_proj_a[l], w_proj_b[l], w_out[l], w_ffn_gate[l],
                           w_ffn_up[l], w_ffn_down[l])
        x2 = _layer(x2, bsz, seq, cos, sin, norm1[l], qn_a[l], kn_a[l], gn_b[l], norm2[l], w)
    return x2.reshape(bsz, seq, D_MODEL)
```

```python
import functools

import jax
import jax.numpy as jnp
from jax import lax
from jax.experimental import pallas as pl
from jax.experimental.pallas import tpu as pltpu

F32 = jnp.float32
BF16 = jnp.bfloat16

D_MODEL = 1024
HEAD_DIM = 128
KV_HEADS = 4
DILATIONS = (1, 4, 16)
KEYS_PER_BLOCK = 128
Q_HEADS = KV_HEADS * len(DILATIONS)
ROPE_THETA = 10000.0
GLA_HEADS = 4
GLA_DK = 128
GLA_DV = 256
GLA_LOWRANK = 16
GLA_GATE_TEMP = 16.0
GLA_SUB = 8
GLA_STEP_ROWS = 256
D_FF = 2816
EPS = 1e-6
LOG2_E = 1.4426950408889634

W_QA = Q_HEADS * HEAD_DIM
W_KA = KV_HEADS * HEAD_DIM
W_QB = GLA_HEADS * GLA_DK
W_VB = GLA_HEADS * GLA_DV
IN_SPLITS = (W_QA, W_KA, W_KA, W_QB, W_QB, W_VB, W_VB, GLA_LOWRANK, D_MODEL, D_MODEL)

LANES = 128
MXU_TILE = 256
FFN_BLOCKS = ((0, 6 * MXU_TILE), (6 * MXU_TILE, D_FF))
TOKEN_TILE = 512
VMEM_LIMIT_BYTES = 56 * 1024 * 1024


def _dot(a, b):
    return jnp.dot(a, b, preferred_element_type=F32)


def _dot_nt(a, b):
    return lax.dot_general(a, b, (((1,), (1,)), ((), ())), preferred_element_type=F32)


def _dot_tn(a, b):
    return lax.dot_general(a, b, (((0,), (0,)), ((), ())), preferred_element_type=F32)


def _rms(t, gain):
    return t * lax.rsqrt(jnp.mean(t * t, axis=-1, keepdims=True) + EPS) * gain


def _sigmoid(t):
    return 1.0 / (1.0 + jnp.exp(-t))


def _const_spec(shape):
    return pl.BlockSpec(shape, lambda *_: (0,) * len(shape), pipeline_mode=pl.Buffered(1))


def _rope_kernel(pos_ref, invf_ref, sign_ref, cos_ref, sin_ref):
    ang = pos_ref[...].astype(F32) * invf_ref[...]
    cos_ref[...] = jnp.cos(ang)
    sin_ref[...] = jnp.sin(ang) * sign_ref[...]


def _rope_tables(positions):
    n_tok = positions.size
    inv_freq = ROPE_THETA ** (-jnp.arange(0, HEAD_DIM, 2, dtype=F32) / HEAD_DIM)
    invf = jnp.concatenate([inv_freq, inv_freq])[None, :]
    sign = jnp.concatenate([-jnp.ones((HEAD_DIM // 2,), F32), jnp.ones((HEAD_DIM // 2,), F32)])[None, :]
    tm = 2048
    return pl.pallas_call(
        _rope_kernel,
        grid=(n_tok // tm,),
        in_specs=[pl.BlockSpec((tm, 1), lambda i: (i, 0)), _const_spec((1, HEAD_DIM)), _const_spec((1, HEAD_DIM))],
        out_specs=[pl.BlockSpec((tm, HEAD_DIM), lambda i: (i, 0))] * 2,
        out_shape=[jax.ShapeDtypeStruct((n_tok, HEAD_DIM), F32)] * 2,
        name="rope_tables",
    )(positions.reshape(n_tok, 1), invf, sign)


def _inproj_gla_kernel(x_ref, g1_ref, cos_ref, sin_ref, qn_ref, kn_ref, gnb_ref, wqa_ref, wka_ref, wva_ref,
                       wqb_ref, wkb_ref, wvb_ref, wrb_ref, wab_ref, wau_ref, ba_ref, wga_ref, wgb_ref,
                       sel_ref, tri_ref, mask_ref, qa_o, ka_o, va_o, ob_o, ga_o, gb_o, x_scr, st_scr,
                       *, tiles_per_seq):
    tm = x_ref.shape[0]

    @pl.when(pl.program_id(0) % tiles_per_seq == 0)
    def _():
        st_scr[...] = jnp.zeros_like(st_scr)

    h = _rms(x_ref[...], g1_ref[...]).astype(BF16)
    cos = cos_ref[...]
    sin = sin_ref[...]

    def norm_rope(t, gain, scale):
        t = _rms(t, gain)
        t = t * cos + pltpu.roll(t, HEAD_DIM // 2, 1) * sin
        return t * scale

    def per_head(w_ref, out_ref, gain, scale):
        width = w_ref.shape[1]
        step = 4 * HEAD_DIM
        for c0 in range(0, width, step):
            t = _dot(h, w_ref[:, c0:c0 + step])
            for j in range(0, step, HEAD_DIM):
                out_ref[:, c0 + j:c0 + j + HEAD_DIM] = norm_rope(t[:, j:j + HEAD_DIM], gain, scale)

    qb = (_dot(h, wqb_ref[...]) * (GLA_DK ** -0.5)).astype(BF16)
    kb = _dot(h, wkb_ref[...]).astype(BF16)
    ab = _dot(h, wab_ref[...]).astype(BF16)
    z = _dot(ab, wau_ref[...]) + ba_ref[...]
    la = (jnp.minimum(z, 0.0) - jnp.log1p(jnp.exp(-jnp.abs(z)))) * (LOG2_E / GLA_GATE_TEMP)
    vb = _dot(h, wvb_ref[...]).astype(BF16)
    r = _dot(h, wrb_ref[...])
    rb = (r * _sigmoid(r)).astype(BF16)
    per_head(wqa_ref, qa_o, qn_ref[...], HEAD_DIM ** -0.5 * LOG2_E)
    per_head(wka_ref, ka_o, kn_ref[...], 1.0)
    gnb = gnb_ref[...]
    n_slots = x_scr.shape[0]
    for blk in range(tm // GLA_STEP_ROWS):
        rows = slice(blk * GLA_STEP_ROWS, (blk + 1) * GLA_STEP_ROWS)
        for head in range(GLA_HEADS):
            kcols = slice(head * GLA_DK, (head + 1) * GLA_DK)
            vcols = slice(head * GLA_DV, (head + 1) * GLA_DV)
            y, st = _gla_block(qb[rows, kcols], kb[rows, kcols], vb[rows, vcols], la[rows, kcols], rb[rows, vcols],
                               gnb, st_scr[head], sel_ref, tri_ref, mask_ref,
                               x_scr.at[(blk * GLA_HEADS + head) % n_slots])
            ob_o[rows, vcols] = y
            st_scr[head] = st

    va_o[...] = _dot(h, wva_ref[...])
    for c0 in range(0, D_MODEL, 512):
        ga_o[:, c0:c0 + 512] = _sigmoid(_dot(h, wga_ref[:, c0:c0 + 512])).astype(BF16)
        gb_o[:, c0:c0 + 512] = _sigmoid(_dot(h, wgb_ref[:, c0:c0 + 512])).astype(BF16)


def _inproj_gla(x2, seq, cos, sin, g1, qn, kn, gnb, w, tm):
    n_tok = x2.shape[0]
    row = lambda width: pl.BlockSpec((tm, width), lambda i: (i, 0))
    weights = [w["qa"], w["ka"], w["va"], w["qb"], w["kb"], w["vb"], w["rb"], w["ab"], w["au"], w["ba"], w["ga"], w["gb"]]
    consts = list(_gla_constants())
    in_specs = ([row(D_MODEL), _const_spec((1, D_MODEL)), row(HEAD_DIM), row(HEAD_DIM),
                 _const_spec((1, HEAD_DIM)), _const_spec((1, HEAD_DIM)), _const_spec((1, GLA_DV))]
                + [_const_spec(a.shape) for a in weights + consts])
    outs = [(W_QA, F32), (W_KA, F32), (W_KA, F32), (W_VB, BF16), (D_MODEL, BF16), (D_MODEL, BF16)]
    return pl.pallas_call(
        functools.partial(_inproj_gla_kernel, tiles_per_seq=seq // tm),
        grid=(n_tok // tm,),
        in_specs=in_specs,
        out_specs=[row(width) for width, _ in outs],
        out_shape=[jax.ShapeDtypeStruct((n_tok, width), dt) for width, dt in outs],
        scratch_shapes=[
            pltpu.VMEM((GLA_PRODUCT_SLOTS, GLA_STEP_ROWS, GLA_SUB * GLA_DK), BF16),
            pltpu.VMEM((GLA_HEADS, GLA_DV, GLA_DK), F32),
        ],
        compiler_params=pltpu.CompilerParams(dimension_semantics=("arbitrary",), vmem_limit_bytes=VMEM_LIMIT_BYTES),
        name="inproj_gla",
    )(x2, g1, cos, sin, qn, kn, gnb, *weights, *consts)


ATTN_BLOCKS_PER_BATCH = 16


def _attn_kernel(q0_ref, q1_ref, q2_ref, k_ref, v_ref, o_ref, out_scr, lse_scr):
    seq = k_ref.shape[0]
    nk = KEYS_PER_BLOCK
    qi = lax.broadcasted_iota(jnp.int32, (nk, 2 * nk), 0) + nk
    ki = lax.broadcasted_iota(jnp.int32, (nk, 2 * nk), 1)
    dist = qi - ki
    band_mask = (dist >= 0) & (dist <= nk)
    qc = lax.broadcasted_iota(jnp.int32, (nk, nk), 0)
    kc = lax.broadcasted_iota(jnp.int32, (nk, nk), 1)
    causal_mask = qc >= kc

    def rows(start, size, stride):
        return pl.ds(start, size) if stride == 1 else pl.ds(start, size, stride=stride)

    q_refs = (q0_ref, q1_ref, q2_ref)

    def blocks(todo, n_keys, mask):
        strides = [DILATIONS[g] for g, _, _ in todo]
        q = jnp.stack([q_refs[g][rows(qs, nk, d), :] for (g, qs, _), d in zip(todo, strides)]).astype(BF16)
        k = jnp.stack([k_ref[rows(ks, n_keys, d), :] for (_, _, ks), d in zip(todo, strides)]).astype(BF16)
        v = jnp.stack([v_ref[rows(ks, n_keys, d), :] for (_, _, ks), d in zip(todo, strides)]).astype(BF16)
        s = lax.dot_general(q, k, (((2,), (2,)), ((0,), (0,))), preferred_element_type=F32)
        s = jnp.where(mask[None], s, -jnp.inf)
        m = jnp.max(s, axis=-1, keepdims=True)
        p = jnp.exp2(s - m)
        den = jnp.sum(p, axis=-1, keepdims=True)
        o = lax.dot_general(p.astype(BF16), v, (((2,), (1,)), ((0,), (0,))), preferred_element_type=F32) / den
        lse = jnp.broadcast_to(m + jnp.log2(den), o.shape)
        for u, ((g, qs, _), d) in enumerate(zip(todo, strides)):
            out_scr[g, rows(qs, nk, d), :] = o[u]
            lse_scr[g, rows(qs, nk, d), :] = lse[u]

    first, later = [], []
    for g, stride in enumerate(DILATIONS):
        span = nk * stride
        for r in range(stride):
            first.append((g, r, r))
            later += [(g, r + span * n, r + span * (n - 1)) for n in range(1, seq // span)]
    for todo, n_keys, mask in ((first, nk, causal_mask), (later, 2 * nk, band_mask)):
        n_batches = -(-len(todo) // ATTN_BLOCKS_PER_BATCH)
        size = -(-len(todo) // n_batches)
        for start in range(0, len(todo), size):
            blocks(todo[start:start + size], n_keys, mask)

    chunk = 256

    def combine(i, carry):
        r0 = pl.multiple_of(i * chunk, chunk)
        l0 = lse_scr[0, pl.ds(r0, chunk), :]
        l1 = lse_scr[1, pl.ds(r0, chunk), :]
        l2 = lse_scr[2, pl.ds(r0, chunk), :]
        mx = jnp.maximum(jnp.maximum(l0, l1), l2)
        w0 = jnp.exp2(l0 - mx)
        w1 = jnp.exp2(l1 - mx)
        w2 = jnp.exp2(l2 - mx)
        num = (w0 * out_scr[0, pl.ds(r0, chunk), :] + w1 * out_scr[1, pl.ds(r0, chunk), :]
               + w2 * out_scr[2, pl.ds(r0, chunk), :])
        o_ref[pl.ds(r0, chunk), :] = (num / (w0 + w1 + w2)).astype(o_ref.dtype)
        return carry

    lax.fori_loop(0, seq // chunk, combine, 0)


def _attention(qa, ka, va):
    bsz, seq, _ = ka.shape
    n_groups = len(DILATIONS)

    def q_spec(group):
        return pl.BlockSpec((None, seq, HEAD_DIM), lambda b, h: (b, 0, group * KV_HEADS + h))

    kv_spec = pl.BlockSpec((None, seq, HEAD_DIM), lambda b, h: (b, 0, h))
    return pl.pallas_call(
        _attn_kernel,
        grid=(bsz, KV_HEADS),
        in_specs=[q_spec(0), q_spec(1), q_spec(2), kv_spec, kv_spec],
        out_specs=kv_spec,
        out_shape=jax.ShapeDtypeStruct((bsz, seq, KV_HEADS * HEAD_DIM), BF16),
        scratch_shapes=[pltpu.VMEM((n_groups, seq, HEAD_DIM), F32), pltpu.VMEM((n_groups, seq, HEAD_DIM), F32)],
        compiler_params=pltpu.CompilerParams(dimension_semantics=("arbitrary", "arbitrary"),
                                             vmem_limit_bytes=VMEM_LIMIT_BYTES),
        name="dilated_attention",
    )(qa, qa, qa, ka, va)


GLA_HALVES = tuple(h for h in (8, 16, 32, 64, 128, 256, 512) if GLA_SUB <= h < GLA_STEP_ROWS)


GLA_PRODUCT_SLOTS = 4


def _gla_block(q, k, v, la, gate, gn, st, sel_ref, tri_ref, mask_ref, x_ref):
    rows = GLA_STEP_ROWS
    n_subs = rows // GLA_SUB
    hi = la.astype(BF16)
    rest = la - hi.astype(F32)
    mid = rest.astype(BF16)
    low = (rest - mid.astype(F32)).astype(BF16)
    cum = _dot(tri_ref[...], jnp.concatenate([hi, mid, low], axis=1))
    b = cum[:, :GLA_DK] + cum[:, GLA_DK:2 * GLA_DK] + cum[:, 2 * GLA_DK:]
    q = q.astype(F32)
    k = k.astype(F32)
    b_last = b[rows - 1:rows, :]
    qh = (q * jnp.exp2(b)).astype(BF16)
    kh = (k * jnp.exp2(b_last - b)).astype(BF16)
    sub_row = lax.broadcasted_iota(jnp.int32, (n_subs, GLA_SUB, GLA_DK), 1)
    qs = q.reshape(n_subs, GLA_SUB, GLA_DK)
    ks = k.reshape(n_subs, GLA_SUB, GLA_DK)
    bs = b.reshape(n_subs, GLA_SUB, GLA_DK)
    for j in range(GLA_SUB):
        decay = jnp.exp2(jnp.where(sub_row >= j, bs - bs[:, j:j + 1, :], -jnp.inf))
        xj = qs * ks[:, j:j + 1, :] * decay
        x_ref[:, j * GLA_DK:(j + 1) * GLA_DK] = xj.reshape(rows, GLA_DK).astype(BF16)
    att = mask_ref[0] * _dot(x_ref[...], sel_ref[...])
    for level, h in enumerate(GLA_HALVES):
        shape = (rows // (2 * h), 2 * h, GLA_DK)
        bq, q3, k3 = b.reshape(shape), q.reshape(shape), k.reshape(shape)
        b_mid = bq[:, h - 1:h, :]
        q_up = q3[:, h:, :] * jnp.exp2(bq[:, h:, :] - b_mid)
        k_lo = k3[:, :h, :] * jnp.exp2(b_mid - bq[:, :h, :])
        q_l = jnp.concatenate([q3[:, :h, :], q_up], axis=1).reshape(rows, GLA_DK).astype(BF16)
        k_l = jnp.concatenate([k_lo, k3[:, h:, :]], axis=1).reshape(rows, GLA_DK).astype(BF16)
        att = att + mask_ref[level + 1] * _dot_nt(q_l, k_l)
    o = _dot_nt(qh, st.astype(BF16)) + _dot(att.astype(BF16), v)
    st_new = st * jnp.exp2(b_last) + _dot_tn(v, kh)
    y = _rms(o, gn) * gate.astype(F32)
    return y.astype(BF16), st_new


def _gla_constants():
    rows = GLA_STEP_ROWS
    j = jnp.arange(GLA_SUB * GLA_DK)[:, None] // GLA_DK
    c = jnp.arange(rows)[None, :]
    sel = (c % GLA_SUB == j).astype(BF16)
    t = jnp.arange(rows)[:, None]
    s = jnp.arange(rows)[None, :]
    tri = (s <= t).astype(BF16)
    masks = [(t // GLA_SUB == s // GLA_SUB) & (s <= t)]
    for h in GLA_HALVES:
        masks.append((t // (2 * h) == s // (2 * h)) & (t % (2 * h) >= h) & (s % (2 * h) < h))
    return sel, tri, jnp.stack(masks).astype(F32)


def _merge_ffn_kernel(x_ref, oa_ref, ob_ref, ga_ref, gb_ref, pa_ref, pb_ref, wo_ref, g2_ref, wg_ref, wu_ref,
                      wd_ref, o_ref, acc_scr):
    y = (ga_ref[...].astype(F32) * _dot(oa_ref[...], pa_ref[...])
         + gb_ref[...].astype(F32) * _dot(ob_ref[...], pb_ref[...]))
    x1 = x_ref[...] + _dot(y.astype(BF16), wo_ref[...])
    h2 = _rms(x1, g2_ref[...]).astype(BF16)
    acc_scr[...] = x1
    for lo, hi in FFN_BLOCKS:
        gate = _dot(h2, wg_ref[:, lo:hi])
        up = _dot(h2, wu_ref[:, lo:hi])
        act = (gate * _sigmoid(gate) * up).astype(BF16)
        acc_scr[...] += _dot(act, wd_ref[lo:hi, :])
    o_ref[...] = acc_scr[...]


def _merge_ffn(x2, oa, ob, ga, gb, w, g2, tm):
    n_tok = x2.shape[0]
    row = lambda width: pl.BlockSpec((tm, width), lambda i: (i, 0))
    weights = [w["pa"], w["pb"], w["wo"], g2, w["wg"], w["wu"], w["wd"]]
    return pl.pallas_call(
        _merge_ffn_kernel,
        grid=(n_tok // tm,),
        in_specs=[row(D_MODEL), row(W_KA), row(W_VB), row(D_MODEL), row(D_MODEL)]
        + [_const_spec(a.shape) for a in weights],
        out_specs=row(D_MODEL),
        out_shape=jax.ShapeDtypeStruct((n_tok, D_MODEL), F32),
        scratch_shapes=[pltpu.VMEM((tm, D_MODEL), F32)],
        compiler_params=pltpu.CompilerParams(dimension_semantics=("arbitrary",), vmem_limit_bytes=VMEM_LIMIT_BYTES),
        name="merge_ffn",
    )(x2, oa, ob, ga, gb, *weights)


def _layer_weights(w_in, w_a_up, b_a, w_proj_a, w_proj_b, w_out, w_ffn_gate, w_ffn_up, w_ffn_down):
    pieces = {}
    start = 0
    for name, width in zip(("qa", "ka", "va", "qb", "kb", "vb", "rb", "ab", "ga", "gb"), IN_SPLITS):
        pieces[name] = w_in[:, start:start + width].astype(BF16)
        start += width
    pieces["ab"] = jnp.pad(pieces["ab"], ((0, 0), (0, LANES - GLA_LOWRANK)))
    pieces["au"] = jnp.pad(w_a_up.astype(BF16), ((0, LANES - GLA_LOWRANK), (0, 0)))
    pieces["ba"] = b_a[None, :]
    pieces["pa"] = w_proj_a.astype(BF16)
    pieces["pb"] = w_proj_b.astype(BF16)
    pieces["wo"] = w_out.astype(BF16)
    pieces["wg"] = w_ffn_gate.astype(BF16)
    pieces["wu"] = w_ffn_up.astype(BF16)
    pieces["wd"] = w_ffn_down.astype(BF16)
    return pieces


def _layer(x2, bsz, seq, cos, sin, norm1, qn_a, kn_a, gn_b, norm2, w):
    qa, ka, va, ob, ga, gb = _inproj_gla(x2, seq, cos, sin, norm1[None, :], qn_a[None, :], kn_a[None, :],
                                         gn_b[None, :], w, tm=TOKEN_TILE)
    shape3 = lambda a: a.reshape(bsz, seq, a.shape[-1])
    oa = _attention(shape3(qa), shape3(ka), shape3(va))
    return _merge_ffn(x2, oa.reshape(bsz * seq, -1), ob, ga, gb, w, norm2[None, :], tm=TOKEN_TILE)


def kernel(x, positions, norm1, w_in, qn_a, kn_a, w_a_up, b_a, gn_b, w_proj_a, w_proj_b, w_out, norm2,
           w_ffn_gate, w_ffn_up, w_ffn_down):
    bsz, seq, _ = x.shape
    cos, sin = _rope_tables(positions)
    x2 = x.reshape(bsz * seq, D_MODEL)
    for l in range(w_in.shape[0]):
        w = _layer_weights(w_in[l], w_a_up[l], b_a[l], w_proj_a[l], w_proj_b[l], w_out[l], w_ffn_gate[l],
                           w_ffn_up[l], w_ffn_down[l])
        x2 = _layer(x2, bsz, seq, cos, sin, norm1[l], qn_a[l], kn_a[l], gn_b[l], norm2[l], w)
    return x2.reshape(bsz, seq, D_MODEL)
```

```python
import functools

import jax
import jax.numpy as jnp
from jax import lax
from jax.experimental import pallas as pl
from jax.experimental.pallas import tpu as pltpu

F32 = jnp.float32
BF16 = jnp.bfloat16

D_MODEL = 1024
HEAD_DIM = 128
KV_HEADS = 4
DILATIONS = (1, 4, 16)
KEYS_PER_BLOCK = 128
Q_HEADS = KV_HEADS * len(DILATIONS)
ROPE_THETA = 10000.0
GLA_HEADS = 4
GLA_DK = 128
GLA_DV = 256
GLA_LOWRANK = 16
GLA_GATE_TEMP = 16.0
GLA_SUB = 8
GLA_STEP_ROWS = 256
D_FF = 2816
EPS = 1e-6
LOG2_E = 1.4426950408889634

W_QA = Q_HEADS * HEAD_DIM
W_KA = KV_HEADS * HEAD_DIM
W_QB = GLA_HEADS * GLA_DK
W_VB = GLA_HEADS * GLA_DV
IN_SPLITS = (W_QA, W_KA, W_KA, W_QB, W_QB, W_VB, W_VB, GLA_LOWRANK, D_MODEL, D_MODEL)

LANES = 128
MXU_TILE = 256
FFN_BLOCKS = ((0, 6 * MXU_TILE), (6 * MXU_TILE, D_FF))
TOKEN_TILE = 512
VMEM_LIMIT_BYTES = 56 * 1024 * 1024


def _dot(a, b):
    return jnp.dot(a, b, preferred_element_type=F32)


def _dot_nt(a, b):
    return lax.dot_general(a, b, (((1,), (1,)), ((), ())), preferred_element_type=F32)


def _dot_tn(a, b):
    return lax.dot_general(a, b, (((0,), (0,)), ((), ())), preferred_element_type=F32)


def _rms(t, gain):
    return t * lax.rsqrt(jnp.mean(t * t, axis=-1, keepdims=True) + EPS) * gain


def _sigmoid(t):
    return 1.0 / (1.0 + jnp.exp(-t))


def _const_spec(shape):
    return pl.BlockSpec(shape, lambda *_: (0,) * len(shape), pipeline_mode=pl.Buffered(1))


def _rope_kernel(pos_ref, invf_ref, sign_ref, cos_ref, sin_ref):
    ang = pos_ref[...].astype(F32) * invf_ref[...]
    cos_ref[...] = jnp.cos(ang)
    sin_ref[...] = jnp.sin(ang) * sign_ref[...]


def _rope_tables(positions):
    n_tok = positions.size
    inv_freq = ROPE_THETA ** (-jnp.arange(0, HEAD_DIM, 2, dtype=F32) / HEAD_DIM)
    invf = jnp.concatenate([inv_freq, inv_freq])[None, :]
    sign = jnp.concatenate([-jnp.ones((HEAD_DIM // 2,), F32), jnp.ones((HEAD_DIM // 2,), F32)])[None, :]
    tm = 2048
    return pl.pallas_call(
        _rope_kernel,
        grid=(n_tok // tm,),
        in_specs=[pl.BlockSpec((tm, 1), lambda i: (i, 0)), _const_spec((1, HEAD_DIM)), _const_spec((1, HEAD_DIM))],
        out_specs=[pl.BlockSpec((tm, HEAD_DIM), lambda i: (i, 0))] * 2,
        out_shape=[jax.ShapeDtypeStruct((n_tok, HEAD_DIM), F32)] * 2,
        name="rope_tables",
    )(positions.reshape(n_tok, 1), invf, sign)


def _inproj_gla_kernel(x_ref, g1_ref, cos_ref, sin_ref, qn_ref, kn_ref, gnb_ref, wqa_ref, wka_ref, wva_ref,
                       wqb_ref, wkb_ref, wvb_ref, wrb_ref, wab_ref, wau_ref, ba_ref, wga_ref, wgb_ref,
                       sel_ref, tri_ref, mask_ref, qa_o, ka_o, va_o, ob_o, ga_o, gb_o, x_scr, st_scr,
                       *, tiles_per_seq):
    tm = x_ref.shape[0]

    @pl.when(pl.program_id(0) % tiles_per_seq == 0)
    def _():
        st_scr[...] = jnp.zeros_like(st_scr)

    h = _rms(x_ref[...], g1_ref[...]).astype(BF16)
    cos = cos_ref[...]
    sin = sin_ref[...]

    def head_segment(w_ref, out_ref, gain, scale, c0):
        width = 4 * HEAD_DIM
        t = _dot(h, w_ref[:, c0:c0 + width])
        for j in range(0, width, HEAD_DIM):
            u = _rms(t[:, j:j + HEAD_DIM], gain)
            u = u * cos + pltpu.roll(u, HEAD_DIM // 2, 1) * sin
            out_ref[:, c0 + j:c0 + j + HEAD_DIM] = u * scale

    def gate_segment(w_ref, out_ref, c0):
        out_ref[:, c0:c0 + 512] = _sigmoid(_dot(h, w_ref[:, c0:c0 + 512])).astype(BF16)

    def value_segment():
        va_o[...] = _dot(h, wva_ref[...])

    q_scale = HEAD_DIM ** -0.5 * LOG2_E
    segments = [functools.partial(head_segment, wqa_ref, qa_o, qn_ref[...], q_scale, c0)
                for c0 in range(0, W_QA, 4 * HEAD_DIM)]
    segments.append(functools.partial(head_segment, wka_ref, ka_o, kn_ref[...], 1.0, 0))
    segments.append(value_segment)
    segments += [functools.partial(gate_segment, w_ref, out_ref, c0)
                 for w_ref, out_ref in ((wga_ref, ga_o), (wgb_ref, gb_o)) for c0 in range(0, D_MODEL, 512)]

    qb = (_dot(h, wqb_ref[...]) * (GLA_DK ** -0.5)).astype(BF16)
    kb = _dot(h, wkb_ref[...]).astype(BF16)
    ab = _dot(h, wab_ref[...]).astype(BF16)
    z = _dot(ab, wau_ref[...]) + ba_ref[...]
    la = (jnp.minimum(z, 0.0) - jnp.log1p(jnp.exp(-jnp.abs(z)))) * (LOG2_E / GLA_GATE_TEMP)
    vb = _dot(h, wvb_ref[...]).astype(BF16)
    r = _dot(h, wrb_ref[...])
    rb = (r * _sigmoid(r)).astype(BF16)
    gnb = gnb_ref[...]
    n_slots = x_scr.shape[0]
    segments[0]()
    next_segment = 1
    for blk in range(tm // GLA_STEP_ROWS):
        rows = slice(blk * GLA_STEP_ROWS, (blk + 1) * GLA_STEP_ROWS)
        for head in range(GLA_HEADS):
            kcols = slice(head * GLA_DK, (head + 1) * GLA_DK)
            vcols = slice(head * GLA_DV, (head + 1) * GLA_DV)
            y, st = _gla_block(qb[rows, kcols], kb[rows, kcols], vb[rows, vcols], la[rows, kcols], rb[rows, vcols],
                               gnb, st_scr[head], sel_ref, tri_ref, mask_ref,
                               x_scr.at[(blk * GLA_HEADS + head) % n_slots])
            ob_o[rows, vcols] = y
            st_scr[head] = st
            if next_segment < len(segments):
                segments[next_segment]()
                next_segment += 1
    for segment in segments[next_segment:]:
        segment()


def _inproj_gla(x2, seq, cos, sin, g1, qn, kn, gnb, w, tm):
    n_tok = x2.shape[0]
    row = lambda width: pl.BlockSpec((tm, width), lambda i: (i, 0))
    weights = [w["qa"], w["ka"], w["va"], w["qb"], w["kb"], w["vb"], w["rb"], w["ab"], w["au"], w["ba"], w["ga"], w["gb"]]
    consts = list(_gla_constants())
    in_specs = ([row(D_MODEL), _const_spec((1, D_MODEL)), row(HEAD_DIM), row(HEAD_DIM),
                 _const_spec((1, HEAD_DIM)), _const_spec((1, HEAD_DIM)), _const_spec((1, GLA_DV))]
                + [_const_spec(a.shape) for a in weights + consts])
    outs = [(W_QA, F32), (W_KA, F32), (W_KA, F32), (W_VB, BF16), (D_MODEL, BF16), (D_MODEL, BF16)]
    return pl.pallas_call(
        functools.partial(_inproj_gla_kernel, tiles_per_seq=seq // tm),
        grid=(n_tok // tm,),
        in_specs=in_specs,
        out_specs=[row(width) for width, _ in outs],
        out_shape=[jax.ShapeDtypeStruct((n_tok, width), dt) for width, dt in outs],
        scratch_shapes=[
            pltpu.VMEM((GLA_PRODUCT_SLOTS, GLA_STEP_ROWS, GLA_SUB * GLA_DK), BF16),
            pltpu.VMEM((GLA_HEADS, GLA_DV, GLA_DK), F32),
        ],
        compiler_params=pltpu.CompilerParams(dimension_semantics=("arbitrary",), vmem_limit_bytes=VMEM_LIMIT_BYTES),
        name="inproj_gla",
    )(x2, g1, cos, sin, qn, kn, gnb, *weights, *consts)


ATTN_BLOCKS_PER_BATCH = 11


def _attn_kernel(q0_ref, q1_ref, q2_ref, k_ref, v_ref, o_ref, out_scr, lse_scr):
    seq = k_ref.shape[0]
    nk = KEYS_PER_BLOCK
    qi = lax.broadcasted_iota(jnp.int32, (nk, 2 * nk), 0) + nk
    ki = lax.broadcasted_iota(jnp.int32, (nk, 2 * nk), 1)
    dist = qi - ki
    band_mask = (dist >= 0) & (dist <= nk)
    qc = lax.broadcasted_iota(jnp.int32, (nk, nk), 0)
    kc = lax.broadcasted_iota(jnp.int32, (nk, nk), 1)
    causal_mask = qc >= kc

    def rows(start, size, stride):
        return pl.ds(start, size) if stride == 1 else pl.ds(start, size, stride=stride)

    q_refs = (q0_ref, q1_ref, q2_ref)

    def blocks(todo, n_keys, mask):
        strides = [DILATIONS[g] for g, _, _ in todo]
        q = jnp.stack([q_refs[g][rows(qs, nk, d), :] for (g, qs, _), d in zip(todo, strides)]).astype(BF16)
        k = jnp.stack([k_ref[rows(ks, n_keys, d), :] for (_, _, ks), d in zip(todo, strides)]).astype(BF16)
        v = jnp.stack([v_ref[rows(ks, n_keys, d), :] for (_, _, ks), d in zip(todo, strides)]).astype(BF16)
        s = lax.dot_general(q, k, (((2,), (2,)), ((0,), (0,))), preferred_element_type=F32)
        s = jnp.where(mask[None], s, -jnp.inf)
        m = jnp.max(s, axis=-1, keepdims=True)
        p = jnp.exp2(s - m)
        den = jnp.sum(p, axis=-1, keepdims=True)
        o = lax.dot_general(p.astype(BF16), v, (((2,), (1,)), ((0,), (0,))), preferred_element_type=F32) / den
        lse = jnp.broadcast_to(m + jnp.log2(den), o.shape)
        for u, ((g, qs, _), d) in enumerate(zip(todo, strides)):
            out_scr[g, rows(qs, nk, d), :] = o[u]
            lse_scr[g, rows(qs, nk, d), :] = lse[u]

    first, later = [], []
    for g, stride in enumerate(DILATIONS):
        span = nk * stride
        for r in range(stride):
            first.append((g, r, r))
            later += [(g, r + span * n, r + span * (n - 1)) for n in range(1, seq // span)]
    for todo, n_keys, mask in ((first, nk, causal_mask), (later, 2 * nk, band_mask)):
        n_batches = -(-len(todo) // ATTN_BLOCKS_PER_BATCH)
        size = -(-len(todo) // n_batches)
        for start in range(0, len(todo), size):
            blocks(todo[start:start + size], n_keys, mask)

    chunk = 256

    def combine(i, carry):
        r0 = pl.multiple_of(i * chunk, chunk)
        l0 = lse_scr[0, pl.ds(r0, chunk), :]
        l1 = lse_scr[1, pl.ds(r0, chunk), :]
        l2 = lse_scr[2, pl.ds(r0, chunk), :]
        mx = jnp.maximum(jnp.maximum(l0, l1), l2)
        w0 = jnp.exp2(l0 - mx)
        w1 = jnp.exp2(l1 - mx)
        w2 = jnp.exp2(l2 - mx)
        num = (w0 * out_scr[0, pl.ds(r0, chunk), :] + w1 * out_scr[1, pl.ds(r0, chunk), :]
               + w2 * out_scr[2, pl.ds(r0, chunk), :])
        o_ref[pl.ds(r0, chunk), :] = (num / (w0 + w1 + w2)).astype(o_ref.dtype)
        return carry

    lax.fori_loop(0, seq // chunk, combine, 0)


def _attention(qa, ka, va):
    bsz, seq, _ = ka.shape
    n_groups = len(DILATIONS)

    def q_spec(group):
        return pl.BlockSpec((None, seq, HEAD_DIM), lambda b, h: (b, 0, group * KV_HEADS + h))

    kv_spec = pl.BlockSpec((None, seq, HEAD_DIM), lambda b, h: (b, 0, h))
    return pl.pallas_call(
        _attn_kernel,
        grid=(bsz, KV_HEADS),
        in_specs=[q_spec(0), q_spec(1), q_spec(2), kv_spec, kv_spec],
        out_specs=kv_spec,
        out_shape=jax.ShapeDtypeStruct((bsz, seq, KV_HEADS * HEAD_DIM), BF16),
        scratch_shapes=[pltpu.VMEM((n_groups, seq, HEAD_DIM), F32), pltpu.VMEM((n_groups, seq, HEAD_DIM), F32)],
        compiler_params=pltpu.CompilerParams(dimension_semantics=("arbitrary", "arbitrary"),
                                             vmem_limit_bytes=VMEM_LIMIT_BYTES),
        name="dilated_attention",
    )(qa, qa, qa, ka, va)


GLA_HALVES = tuple(h for h in (8, 16, 32, 64, 128, 256, 512) if GLA_SUB <= h < GLA_STEP_ROWS)
GLA_PRODUCT_SLOTS = 4


def _gla_block(q, k, v, la, gate, gn, st, sel_ref, tri_ref, mask_ref, x_ref):
    rows = GLA_STEP_ROWS
    n_subs = rows // GLA_SUB
    hi = la.astype(BF16)
    rest = la - hi.astype(F32)
    mid = rest.astype(BF16)
    low = (rest - mid.astype(F32)).astype(BF16)
    cum = _dot(tri_ref[...], jnp.concatenate([hi, mid, low], axis=1))
    b = cum[:, :GLA_DK] + cum[:, GLA_DK:2 * GLA_DK] + cum[:, 2 * GLA_DK:]
    q = q.astype(F32)
    k = k.astype(F32)
    b_last = b[rows - 1:rows, :]
    qh = (q * jnp.exp2(b)).astype(BF16)
    kh = (k * jnp.exp2(b_last - b)).astype(BF16)
    sub_row = lax.broadcasted_iota(jnp.int32, (n_subs, GLA_SUB, GLA_DK), 1)
    qs = q.reshape(n_subs, GLA_SUB, GLA_DK)
    ks = k.reshape(n_subs, GLA_SUB, GLA_DK)
    bs = b.reshape(n_subs, GLA_SUB, GLA_DK)
    for j in range(GLA_SUB):
        decay = jnp.exp2(jnp.where(sub_row >= j, bs - bs[:, j:j + 1, :], -jnp.inf))
        xj = qs * ks[:, j:j + 1, :] * decay
        x_ref[:, j * GLA_DK:(j + 1) * GLA_DK] = xj.reshape(rows, GLA_DK).astype(BF16)
    att = mask_ref[0] * _dot(x_ref[...], sel_ref[...]).astype(BF16)
    for level, h in enumerate(GLA_HALVES):
        shape = (rows // (2 * h), 2 * h, GLA_DK)
        bq, q3, k3 = b.reshape(shape), q.reshape(shape), k.reshape(shape)
        b_mid = bq[:, h - 1:h, :]
        q_up = q3[:, h:, :] * jnp.exp2(bq[:, h:, :] - b_mid)
        k_lo = k3[:, :h, :] * jnp.exp2(b_mid - bq[:, :h, :])
        q_l = jnp.concatenate([q3[:, :h, :], q_up], axis=1).reshape(rows, GLA_DK).astype(BF16)
        k_l = jnp.concatenate([k_lo, k3[:, h:, :]], axis=1).reshape(rows, GLA_DK).astype(BF16)
        att = att + mask_ref[level + 1] * _dot_nt(q_l, k_l).astype(BF16)
    o = _dot_nt(qh, st.astype(BF16)) + _dot(att, v)
    st_new = st * jnp.exp2(b_last) + _dot_tn(v, kh)
    y = _rms(o, gn) * gate.astype(F32)
    return y.astype(BF16), st_new


def _gla_constants():
    rows = GLA_STEP_ROWS
    j = jnp.arange(GLA_SUB * GLA_DK)[:, None] // GLA_DK
    c = jnp.arange(rows)[None, :]
    sel = (c % GLA_SUB == j).astype(BF16)
    t = jnp.arange(rows)[:, None]
    s = jnp.arange(rows)[None, :]
    tri = (s <= t).astype(BF16)
    masks = [(t // GLA_SUB == s // GLA_SUB) & (s <= t)]
    for h in GLA_HALVES:
        masks.append((t // (2 * h) == s // (2 * h)) & (t % (2 * h) >= h) & (s % (2 * h) < h))
    return sel, tri, jnp.stack(masks).astype(BF16)


def _merge_ffn_kernel(x_ref, oa_ref, ob_ref, ga_ref, gb_ref, pa_ref, pb_ref, wo_ref, g2_ref, wg_ref, wu_ref,
                      wd_ref, o_ref, acc_scr):
    y = (ga_ref[...].astype(F32) * _dot(oa_ref[...], pa_ref[...])
         + gb_ref[...].astype(F32) * _dot(ob_ref[...], pb_ref[...]))
    x1 = x_ref[...] + _dot(y.astype(BF16), wo_ref[...])
    h2 = _rms(x1, g2_ref[...]).astype(BF16)
    acc_scr[...] = x1
    for lo, hi in FFN_BLOCKS:
        gate = _dot(h2, wg_ref[:, lo:hi])
        up = _dot(h2, wu_ref[:, lo:hi])
        act = (gate * _sigmoid(gate) * up).astype(BF16)
        acc_scr[...] += _dot(act, wd_ref[lo:hi, :])
    o_ref[...] = acc_scr[...]


def _merge_ffn(x2, oa, ob, ga, gb, w, g2, tm):
    n_tok = x2.shape[0]
    row = lambda width: pl.BlockSpec((tm, width), lambda i: (i, 0))
    weights = [w["pa"], w["pb"], w["wo"], g2, w["wg"], w["wu"], w["wd"]]
    return pl.pallas_call(
        _merge_ffn_kernel,
        grid=(n_tok // tm,),
        in_specs=[row(D_MODEL), row(W_KA), row(W_VB), row(D_MODEL), row(D_MODEL)]
        + [_const_spec(a.shape) for a in weights],
        out_specs=row(D_MODEL),
        out_shape=jax.ShapeDtypeStruct((n_tok, D_MODEL), F32),
        scratch_shapes=[pltpu.VMEM((tm, D_MODEL), F32)],
        compiler_params=pltpu.CompilerParams(dimension_semantics=("arbitrary",), vmem_limit_bytes=VMEM_LIMIT_BYTES),
        name="merge_ffn",
    )(x2, oa, ob, ga, gb, *weights)


def _layer_weights(w_in, w_a_up, b_a, w_proj_a, w_proj_b, w_out, w_ffn_gate, w_ffn_up, w_ffn_down):
    pieces = {}
    start = 0
    for name, width in zip(("qa", "ka", "va", "qb", "kb", "vb", "rb", "ab", "ga", "gb"), IN_SPLITS):
        pieces[name] = w_in[:, start:start + width].astype(BF16)
        start += width
    pieces["ab"] = jnp.pad(pieces["ab"], ((0, 0), (0, LANES - GLA_LOWRANK)))
    pieces["au"] = jnp.pad(w_a_up.astype(BF16), ((0, LANES - GLA_LOWRANK), (0, 0)))
    pieces["ba"] = b_a[None, :]
    pieces["pa"] = w_proj_a.astype(BF16)
    pieces["pb"] = w_proj_b.astype(BF16)
    pieces["wo"] = w_out.astype(BF16)
    pieces["wg"] = w_ffn_gate.astype(BF16)
    pieces["wu"] = w_ffn_up.astype(BF16)
    pieces["wd"] = w_ffn_down.astype(BF16)
    return pieces


def _layer(x2, bsz, seq, cos, sin, norm1, qn_a, kn_a, gn_b, norm2, w):
    qa, ka, va, ob, ga, gb = _inproj_gla(x2, seq, cos, sin, norm1[None, :], qn_a[None, :], kn_a[None, :],
                                         gn_b[None, :], w, tm=TOKEN_TILE)
    shape3 = lambda a: a.reshape(bsz, seq, a.shape[-1])
    oa = _attention(shape3(qa), shape3(ka), shape3(va))
    return _merge_ffn(x2, oa.reshape(bsz * seq, -1), ob, ga, gb, w, norm2[None, :], tm=TOKEN_TILE)


def kernel(x, positions, norm1, w_in, qn_a, kn_a, w_a_up, b_a, gn_b, w_proj_a, w_proj_b, w_out, norm2,
           w_ffn_gate, w_ffn_up, w_ffn_down):
    bsz, seq, _ = x.shape
    cos, sin = _rope_tables(positions)
    x2 = x.reshape(bsz * seq, D_MODEL)
    for l in range(w_in.shape[0]):
        w = _layer_weights(w_in[l], w_a_up[l], b_a[l], w_proj_a[l], w_proj_b[l], w_out[l], w_ffn_gate[l],
                           w_ffn_up[l], w_ffn_down[l])
        x2 = _layer(x2, bsz, seq, cos, sin, norm1[l], qn_a[l], kn_a[l], gn_b[l], norm2[l], w)
    return x2.reshape(bsz, seq, D_MODEL)
```

```python
import functools

import jax
import jax.numpy as jnp
from jax import lax
from jax.experimental import pallas as pl
from jax.experimental.pallas import tpu as pltpu

F32 = jnp.float32
BF16 = jnp.bfloat16

D_MODEL = 1024
HEAD_DIM = 128
KV_HEADS = 4
DILATIONS = (1, 4, 16)
KEYS_PER_BLOCK = 128
Q_HEADS = KV_HEADS * len(DILATIONS)
ROPE_THETA = 10000.0
GLA_HEADS = 4
GLA_DK = 128
GLA_DV = 256
GLA_LOWRANK = 16
GLA_GATE_TEMP = 16.0
GLA_SUB = 8
GLA_STEP_ROWS = 256
D_FF = 2816
EPS = 1e-6
LOG2_E = 1.4426950408889634

W_QA = Q_HEADS * HEAD_DIM
W_KA = KV_HEADS * HEAD_DIM
W_QB = GLA_HEADS * GLA_DK
W_VB = GLA_HEADS * GLA_DV
IN_SPLITS = (W_QA, W_KA, W_KA, W_QB, W_QB, W_VB, W_VB, GLA_LOWRANK, D_MODEL, D_MODEL)

LANES = 128
MXU_TILE = 256
FFN_BLOCKS = ((0, 6 * MXU_TILE), (6 * MXU_TILE, D_FF))
TOKEN_TILE = 512
VMEM_LIMIT_BYTES = 56 * 1024 * 1024


def _dot(a, b):
    return jnp.dot(a, b, preferred_element_type=F32)


def _dot_nt(a, b):
    return lax.dot_general(a, b, (((1,), (1,)), ((), ())), preferred_element_type=F32)


def _dot_tn(a, b):
    return lax.dot_general(a, b, (((0,), (0,)), ((), ())), preferred_element_type=F32)


def _rms(t, gain):
    return t * lax.rsqrt(jnp.mean(t * t, axis=-1, keepdims=True) + EPS) * gain


def _sigmoid(t):
    return 1.0 / (1.0 + jnp.exp(-t))


def _const_spec(shape):
    return pl.BlockSpec(shape, lambda *_: (0,) * len(shape), pipeline_mode=pl.Buffered(1))


def _rope_kernel(pos_ref, invf_ref, sign_ref, cos_ref, sin_ref):
    ang = pos_ref[...].astype(F32) * invf_ref[...]
    cos_ref[...] = jnp.cos(ang)
    sin_ref[...] = jnp.sin(ang) * sign_ref[...]


def _rope_tables(positions):
    n_tok = positions.size
    inv_freq = ROPE_THETA ** (-jnp.arange(0, HEAD_DIM, 2, dtype=F32) / HEAD_DIM)
    invf = jnp.concatenate([inv_freq, inv_freq])[None, :]
    sign = jnp.concatenate([-jnp.ones((HEAD_DIM // 2,), F32), jnp.ones((HEAD_DIM // 2,), F32)])[None, :]
    tm = 2048
    return pl.pallas_call(
        _rope_kernel,
        grid=(n_tok // tm,),
        in_specs=[pl.BlockSpec((tm, 1), lambda i: (i, 0)), _const_spec((1, HEAD_DIM)), _const_spec((1, HEAD_DIM))],
        out_specs=[pl.BlockSpec((tm, HEAD_DIM), lambda i: (i, 0))] * 2,
        out_shape=[jax.ShapeDtypeStruct((n_tok, HEAD_DIM), F32)] * 2,
        name="rope_tables",
    )(positions.reshape(n_tok, 1), invf, sign)


def _inproj_gla_kernel(x_ref, g1_ref, cos_ref, sin_ref, qn_ref, kn_ref, gnb_ref, wqa_ref, wka_ref, wva_ref,
                       wqb_ref, wkb_ref, wvb_ref, wrb_ref, wab_ref, wau_ref, ba_ref, wga_ref, wgb_ref,
                       sel_ref, tri_ref, mask_ref, qa_o, ka_o, va_o, ob_o, ga_o, gb_o, x_scr, st_scr,
                       *, tiles_per_seq):
    tm = x_ref.shape[0]

    @pl.when(pl.program_id(0) % tiles_per_seq == 0)
    def _():
        st_scr[...] = jnp.zeros_like(st_scr)

    h = _rms(x_ref[...], g1_ref[...]).astype(BF16)
    cos = cos_ref[...]
    sin = sin_ref[...]

    def head_segment(w_ref, out_ref, gain, scale, c0):
        width = 4 * HEAD_DIM
        t = _dot(h, w_ref[:, c0:c0 + width])
        for j in range(0, width, HEAD_DIM):
            u = _rms(t[:, j:j + HEAD_DIM], gain)
            u = u * cos + pltpu.roll(u, HEAD_DIM // 2, 1) * sin
            out_ref[:, c0 + j:c0 + j + HEAD_DIM] = u * scale

    def gate_segment(w_ref, out_ref, c0):
        out_ref[:, c0:c0 + 512] = _sigmoid(_dot(h, w_ref[:, c0:c0 + 512])).astype(BF16)

    def value_segment():
        va_o[...] = _dot(h, wva_ref[...])

    q_scale = HEAD_DIM ** -0.5 * LOG2_E
    segments = [functools.partial(head_segment, wqa_ref, qa_o, qn_ref[...], q_scale, c0)
                for c0 in range(0, W_QA, 4 * HEAD_DIM)]
    segments.append(functools.partial(head_segment, wka_ref, ka_o, kn_ref[...], 1.0, 0))
    segments.append(value_segment)
    segments += [functools.partial(gate_segment, w_ref, out_ref, c0)
                 for w_ref, out_ref in ((wga_ref, ga_o), (wgb_ref, gb_o)) for c0 in range(0, D_MODEL, 512)]

    qb = (_dot(h, wqb_ref[...]) * (GLA_DK ** -0.5)).astype(BF16)
    kb = _dot(h, wkb_ref[...]).astype(BF16)
    ab = _dot(h, wab_ref[...]).astype(BF16)
    z = _dot(ab, wau_ref[...]) + ba_ref[...]
    la = (jnp.minimum(z, 0.0) - jnp.log1p(jnp.exp(-jnp.abs(z)))) * (LOG2_E / GLA_GATE_TEMP)
    vb = _dot(h, wvb_ref[...]).astype(BF16)
    r = _dot(h, wrb_ref[...])
    rb = (r * _sigmoid(r)).astype(BF16)
    gnb = gnb_ref[...]
    n_slots = x_scr.shape[0]
    segments[0]()
    next_segment = 1
    for blk in range(tm // GLA_STEP_ROWS):
        rows = slice(blk * GLA_STEP_ROWS, (blk + 1) * GLA_STEP_ROWS)
        for head in range(GLA_HEADS):
            kcols = slice(head * GLA_DK, (head + 1) * GLA_DK)
            vcols = slice(head * GLA_DV, (head + 1) * GLA_DV)
            y, st = _gla_block(qb[rows, kcols], kb[rows, kcols], vb[rows, vcols], la[rows, kcols], rb[rows, vcols],
                               gnb, st_scr[head], sel_ref, tri_ref, mask_ref,
                               x_scr.at[(blk * GLA_HEADS + head) % n_slots])
            ob_o[rows, vcols] = y
            st_scr[head] = st
            if next_segment < len(segments):
                segments[next_segment]()
                next_segment += 1
    for segment in segments[next_segment:]:
        segment()


def _inproj_gla(x2, seq, cos, sin, g1, qn, kn, gnb, w, tm):
    n_tok = x2.shape[0]
    row = lambda width: pl.BlockSpec((tm, width), lambda i: (i, 0))
    weights = [w["qa"], w["ka"], w["va"], w["qb"], w["kb"], w["vb"], w["rb"], w["ab"], w["au"], w["ba"], w["ga"], w["gb"]]
    consts = list(_gla_constants())
    in_specs = ([row(D_MODEL), _const_spec((1, D_MODEL)), row(HEAD_DIM), row(HEAD_DIM),
                 _const_spec((1, HEAD_DIM)), _const_spec((1, HEAD_DIM)), _const_spec((1, GLA_DV))]
                + [_const_spec(a.shape) for a in weights + consts])
    outs = [(W_QA, F32), (W_KA, F32), (W_KA, F32), (W_VB, BF16), (D_MODEL, BF16), (D_MODEL, BF16)]
    return pl.pallas_call(
        functools.partial(_inproj_gla_kernel, tiles_per_seq=seq // tm),
        grid=(n_tok // tm,),
        in_specs=in_specs,
        out_specs=[row(width) for width, _ in outs],
        out_shape=[jax.ShapeDtypeStruct((n_tok, width), dt) for width, dt in outs],
        scratch_shapes=[
            pltpu.VMEM((GLA_PRODUCT_SLOTS, GLA_STEP_ROWS, GLA_SUB * GLA_DK), BF16),
            pltpu.VMEM((GLA_HEADS, GLA_DV, GLA_DK), F32),
        ],
        compiler_params=pltpu.CompilerParams(dimension_semantics=("arbitrary",), vmem_limit_bytes=VMEM_LIMIT_BYTES),
        name="inproj_gla",
    )(x2, g1, cos, sin, qn, kn, gnb, *weights, *consts)


ATTN_BLOCKS_PER_BATCH = 11
ATTN_COPY_STRIDE = 4


def _attn_kernel(q0_ref, q1_ref, q2_ref, k_ref, v_ref, o_ref, out_scr, lse_scr, quarter_scr):
    seq = k_ref.shape[0]
    nk = KEYS_PER_BLOCK
    qi = lax.broadcasted_iota(jnp.int32, (nk, 2 * nk), 0) + nk
    ki = lax.broadcasted_iota(jnp.int32, (nk, 2 * nk), 1)
    dist = qi - ki
    band_mask = (dist >= 0) & (dist <= nk)
    qc = lax.broadcasted_iota(jnp.int32, (nk, nk), 0)
    kc = lax.broadcasted_iota(jnp.int32, (nk, nk), 1)
    causal_mask = qc >= kc

    def rows(start, size, stride):
        return pl.ds(start, size) if stride == 1 else pl.ds(start, size, stride=stride)

    sources = (q1_ref, q2_ref, k_ref, v_ref)
    for a, src in enumerate(sources):
        for r4 in range(ATTN_COPY_STRIDE):
            quarter_scr[a, r4] = src[pl.ds(r4, seq // ATTN_COPY_STRIDE, stride=ATTN_COPY_STRIDE), :]

    def load(kind, g, start, size):
        d = DILATIONS[g]
        if d == 1:
            return (q0_ref, k_ref, v_ref)[kind][pl.ds(start, size), :]
        a = (g - 1, 2, 3)[kind]
        r4, first = start % ATTN_COPY_STRIDE, start // ATTN_COPY_STRIDE
        return quarter_scr[a, r4, rows(first, size, d // ATTN_COPY_STRIDE), :]

    def blocks(todo, n_keys, mask):
        strides = [DILATIONS[g] for g, _, _ in todo]
        q = jnp.stack([load(0, g, qs, nk) for g, qs, _ in todo]).astype(BF16)
        k = jnp.stack([load(1, g, ks, n_keys) for g, _, ks in todo]).astype(BF16)
        v = jnp.stack([load(2, g, ks, n_keys) for g, _, ks in todo]).astype(BF16)
        s = lax.dot_general(q, k, (((2,), (2,)), ((0,), (0,))), preferred_element_type=F32)
        s = jnp.where(mask[None], s, -jnp.inf)
        m = jnp.max(s, axis=-1, keepdims=True)
        p = jnp.exp2(s - m)
        den = jnp.sum(p, axis=-1, keepdims=True)
        o = lax.dot_general(p.astype(BF16), v, (((2,), (1,)), ((0,), (0,))), preferred_element_type=F32) / den
        lse = jnp.broadcast_to(m + jnp.log2(den), o.shape)
        for u, ((g, qs, _), d) in enumerate(zip(todo, strides)):
            out_scr[g, rows(qs, nk, d), :] = o[u]
            lse_scr[g, rows(qs, nk, d), :] = lse[u]

    first, later = [], []
    for g, stride in enumerate(DILATIONS):
        span = nk * stride
        for r in range(stride):
            first.append((g, r, r))
            later += [(g, r + span * n, r + span * (n - 1)) for n in range(1, seq // span)]
    for todo, n_keys, mask in ((first, nk, causal_mask), (later, 2 * nk, band_mask)):
        n_batches = -(-len(todo) // ATTN_BLOCKS_PER_BATCH)
        size = -(-len(todo) // n_batches)
        for start in range(0, len(todo), size):
            blocks(todo[start:start + size], n_keys, mask)

    chunk = 256

    def combine(i, carry):
        r0 = pl.multiple_of(i * chunk, chunk)
        l0 = lse_scr[0, pl.ds(r0, chunk), :]
        l1 = lse_scr[1, pl.ds(r0, chunk), :]
        l2 = lse_scr[2, pl.ds(r0, chunk), :]
        mx = jnp.maximum(jnp.maximum(l0, l1), l2)
        w0 = jnp.exp2(l0 - mx)
        w1 = jnp.exp2(l1 - mx)
        w2 = jnp.exp2(l2 - mx)
        num = (w0 * out_scr[0, pl.ds(r0, chunk), :] + w1 * out_scr[1, pl.ds(r0, chunk), :]
               + w2 * out_scr[2, pl.ds(r0, chunk), :])
        o_ref[pl.ds(r0, chunk), :] = (num / (w0 + w1 + w2)).astype(o_ref.dtype)
        return carry

    lax.fori_loop(0, seq // chunk, combine, 0)


def _attention(qa, ka, va):
    bsz, seq, _ = ka.shape
    n_groups = len(DILATIONS)

    def q_spec(group):
        return pl.BlockSpec((None, seq, HEAD_DIM), lambda b, h: (b, 0, group * KV_HEADS + h))

    kv_spec = pl.BlockSpec((None, seq, HEAD_DIM), lambda b, h: (b, 0, h))
    return pl.pallas_call(
        _attn_kernel,
        grid=(bsz, KV_HEADS),
        in_specs=[q_spec(0), q_spec(1), q_spec(2), kv_spec, kv_spec],
        out_specs=kv_spec,
        out_shape=jax.ShapeDtypeStruct((bsz, seq, KV_HEADS * HEAD_DIM), BF16),
        scratch_shapes=[pltpu.VMEM((n_groups, seq, HEAD_DIM), F32), pltpu.VMEM((n_groups, seq, HEAD_DIM), F32),
                        pltpu.VMEM((4, ATTN_COPY_STRIDE, seq // ATTN_COPY_STRIDE, HEAD_DIM), F32)],
        compiler_params=pltpu.CompilerParams(dimension_semantics=("arbitrary", "arbitrary"),
                                             vmem_limit_bytes=VMEM_LIMIT_BYTES),
        name="dilated_attention",
    )(qa, qa, qa, ka, va)


GLA_HALVES = tuple(h for h in (8, 16, 32, 64, 128, 256, 512) if GLA_SUB <= h < GLA_STEP_ROWS)
GLA_PRODUCT_SLOTS = 4


def _gla_block(q, k, v, la, gate, gn, st, sel_ref, tri_ref, mask_ref, x_ref):
    rows = GLA_STEP_ROWS
    n_subs = rows // GLA_SUB
    hi = la.astype(BF16)
    rest = la - hi.astype(F32)
    mid = rest.astype(BF16)
    low = (rest - mid.astype(F32)).astype(BF16)
    cum = _dot(tri_ref[...], jnp.concatenate([hi, mid, low], axis=1))
    b = cum[:, :GLA_DK] + cum[:, GLA_DK:2 * GLA_DK] + cum[:, 2 * GLA_DK:]
    q = q.astype(F32)
    k = k.astype(F32)
    b_last = b[rows - 1:rows, :]
    qh = (q * jnp.exp2(b)).astype(BF16)
    kh = (k * jnp.exp2(b_last - b)).astype(BF16)
    sub_row = lax.broadcasted_iota(jnp.int32, (n_subs, GLA_SUB, GLA_DK), 1)
    qs = q.reshape(n_subs, GLA_SUB, GLA_DK)
    ks = k.reshape(n_subs, GLA_SUB, GLA_DK)
    bs = b.reshape(n_subs, GLA_SUB, GLA_DK)
    for j in range(GLA_SUB):
        decay = jnp.exp2(jnp.where(sub_row >= j, bs - bs[:, j:j + 1, :], -jnp.inf))
        xj = qs * ks[:, j:j + 1, :] * decay
        x_ref[:, j * GLA_DK:(j + 1) * GLA_DK] = xj.reshape(rows, GLA_DK).astype(BF16)
    att = mask_ref[0] * _dot(x_ref[...], sel_ref[...]).astype(BF16)
    for level, h in enumerate(GLA_HALVES):
        shape = (rows // (2 * h), 2 * h, GLA_DK)
        bq, q3, k3 = b.reshape(shape), q.reshape(shape), k.reshape(shape)
        b_mid = bq[:, h - 1:h, :]
        q_up = q3[:, h:, :] * jnp.exp2(bq[:, h:, :] - b_mid)
        k_lo = k3[:, :h, :] * jnp.exp2(b_mid - bq[:, :h, :])
        q_l = jnp.concatenate([q3[:, :h, :], q_up], axis=1).reshape(rows, GLA_DK).astype(BF16)
        k_l = jnp.concatenate([k_lo, k3[:, h:, :]], axis=1).reshape(rows, GLA_DK).astype(BF16)
        att = att + mask_ref[level + 1] * _dot_nt(q_l, k_l).astype(BF16)
    o = _dot_nt(qh, st.astype(BF16)) + _dot(att, v)
    st_new = st * jnp.exp2(b_last) + _dot_tn(v, kh)
    y = _rms(o, gn) * gate.astype(F32)
    return y.astype(BF16), st_new


def _gla_constants():
    rows = GLA_STEP_ROWS
    j = jnp.arange(GLA_SUB * GLA_DK)[:, None] // GLA_DK
    c = jnp.arange(rows)[None, :]
    sel = (c % GLA_SUB == j).astype(BF16)
    t = jnp.arange(rows)[:, None]
    s = jnp.arange(rows)[None, :]
    tri = (s <= t).astype(BF16)
    masks = [(t // GLA_SUB == s // GLA_SUB) & (s <= t)]
    for h in GLA_HALVES:
        masks.append((t // (2 * h) == s // (2 * h)) & (t % (2 * h) >= h) & (s % (2 * h) < h))
    return sel, tri, jnp.stack(masks).astype(BF16)


def _merge_ffn_kernel(x_ref, oa_ref, ob_ref, ga_ref, gb_ref, pa_ref, pb_ref, wo_ref, g2_ref, wg_ref, wu_ref,
                      wd_ref, o_ref, acc_scr):
    y = (ga_ref[...].astype(F32) * _dot(oa_ref[...], pa_ref[...])
         + gb_ref[...].astype(F32) * _dot(ob_ref[...], pb_ref[...]))
    x1 = x_ref[...] + _dot(y.astype(BF16), wo_ref[...])
    h2 = _rms(x1, g2_ref[...]).astype(BF16)
    acc_scr[...] = x1
    for lo, hi in FFN_BLOCKS:
        gate = _dot(h2, wg_ref[:, lo:hi])
        up = _dot(h2, wu_ref[:, lo:hi])
        act = (gate * _sigmoid(gate) * up).astype(BF16)
        acc_scr[...] += _dot(act, wd_ref[lo:hi, :])
    o_ref[...] = acc_scr[...]


def _merge_ffn(x2, oa, ob, ga, gb, w, g2, tm):
    n_tok = x2.shape[0]
    row = lambda width: pl.BlockSpec((tm, width), lambda i: (i, 0))
    weights = [w["pa"], w["pb"], w["wo"], g2, w["wg"], w["wu"], w["wd"]]
    return pl.pallas_call(
        _merge_ffn_kernel,
        grid=(n_tok // tm,),
        in_specs=[row(D_MODEL), row(W_KA), row(W_VB), row(D_MODEL), row(D_MODEL)]
        + [_const_spec(a.shape) for a in weights],
        out_specs=row(D_MODEL),
        out_shape=jax.ShapeDtypeStruct((n_tok, D_MODEL), F32),
        scratch_shapes=[pltpu.VMEM((tm, D_MODEL), F32)],
        compiler_params=pltpu.CompilerParams(dimension_semantics=("arbitrary",), vmem_limit_bytes=VMEM_LIMIT_BYTES),
        name="merge_ffn",
    )(x2, oa, ob, ga, gb, *weights)


def _layer_weights(w_in, w_a_up, b_a, w_proj_a, w_proj_b, w_out, w_ffn_gate, w_ffn_up, w_ffn_down):
    pieces = {}
    start = 0
    for name, width in zip(("qa", "ka", "va", "qb", "kb", "vb", "rb", "ab", "ga", "gb"), IN_SPLITS):
        pieces[name] = w_in[:, start:start + width].astype(BF16)
        start += width
    pieces["ab"] = jnp.pad(pieces["ab"], ((0, 0), (0, LANES - GLA_LOWRANK)))
    pieces["au"] = jnp.pad(w_a_up.astype(BF16), ((0, LANES - GLA_LOWRANK), (0, 0)))
    pieces["ba"] = b_a[None, :]
    pieces["pa"] = w_proj_a.astype(BF16)
    pieces["pb"] = w_proj_b.astype(BF16)
    pieces["wo"] = w_out.astype(BF16)
    pieces["wg"] = w_ffn_gate.astype(BF16)
    pieces["wu"] = w_ffn_up.astype(BF16)
    pieces["wd"] = w_ffn_down.astype(BF16)
    return pieces


def _layer(x2, bsz, seq, cos, sin, norm1, qn_a, kn_a, gn_b, norm2, w):
    qa, ka, va, ob, ga, gb = _inproj_gla(x2, seq, cos, sin, norm1[None, :], qn_a[None, :], kn_a[None, :],
                                         gn_b[None, :], w, tm=TOKEN_TILE)
    shape3 = lambda a: a.reshape(bsz, seq, a.shape[-1])
    oa = _attention(shape3(qa), shape3(ka), shape3(va))
    return _merge_ffn(x2, oa.reshape(bsz * seq, -1), ob, ga, gb, w, norm2[None, :], tm=TOKEN_TILE)


def kernel(x, positions, norm1, w_in, qn_a, kn_a, w_a_up, b_a, gn_b, w_proj_a, w_proj_b, w_out, norm2,
           w_ffn_gate, w_ffn_up, w_ffn_down):
    bsz, seq, _ = x.shape
    cos, sin = _rope_tables(positions)
    x2 = x.reshape(bsz * seq, D_MODEL)
    for l in range(w_in.shape[0]):
        w = _layer_weights(w_in[l], w_a_up[l], b_a[l], w_proj_a[l], w_proj_b[l], w_out[l], w_ffn_gate[l],
                           w_ffn_up[l], w_ffn_down[l])
        x2 = _layer(x2, bsz, seq, cos, sin, norm1[l], qn_a[l], kn_a[l], gn_b[l], norm2[l], w)
    return x2.reshape(bsz, seq, D_MODEL)
```
